```python
import math
import jax, jax.numpy as jnp
from jax import lax
import numpy as np

D_MODEL = 2048
BATCH = 2
SEQ = 16384
DEPTH = 1

PLE_DIM = 256
EPS = 1e-6
SSM_EXPAND = 2
D_INNER = SSM_EXPAND * D_MODEL
SSM_HEAD_DIM = 64
SSM_HEADS = D_INNER // SSM_HEAD_DIM
SSM_GROUPS = 8
HEADS_PER_GROUP = SSM_HEADS // SSM_GROUPS
D_STATE = 128
CONV_WIDTH = 4
CHUNK = 128
CONV_DIM = D_INNER + 2 * SSM_GROUPS * D_STATE
D_POOL = D_MODEL
POOL_WINDOWS = (2, 4, 8, 16)
POOL_GROUPS = len(POOL_WINDOWS)
POOL_GROUP_DIM = D_POOL // POOL_GROUPS
D_FF = ((8 * D_MODEL // 3 + 255) // 256) * 256
IN_PROJ_DIM = D_INNER + CONV_DIM + SSM_HEADS + D_POOL + 2 * D_MODEL

kernel_name = "hybrid_ssd_pool_gated_block"


def rmsnorm(x, g):
    xf = x.astype(jnp.float32)
    y = xf * lax.rsqrt(jnp.mean(xf * xf, axis=-1, keepdims=True) + EPS)
    return (y * g.astype(jnp.float32)).astype(x.dtype)


def causal_depthwise_conv(u, w, b):
    c = u.shape[-1]
    out = lax.conv_general_dilated(
        u, w[:, None, :].astype(u.dtype), window_strides=(1,),
        padding=[(CONV_WIDTH - 1, 0)], dimension_numbers=("NWC", "WIO", "NWC"),
        feature_group_count=c)
    return out + b


def ssd_chunked_scan(xh, dt, a, bm, cm):
    bsz, s = xh.shape[:2]
    nc = s // CHUNK

    def to_chunks(t):
        return jnp.moveaxis(t.reshape((bsz, nc, CHUNK) + t.shape[2:]), 1, 0)

    xc, dtc, bc, cc = map(to_chunks, (xh, dt, bm, cm))
    causal = jnp.tril(jnp.ones((CHUNK, CHUNK), dtype=bool))

    def step(state, inp):
        x_q, dt_q, b_q, c_q = inp
        dt_t = jnp.moveaxis(dt_q, 1, -1)
        cum = jnp.cumsum(dt_t * a[None, :, :, None], axis=-1)
        seg = cum[..., :, None] - cum[..., None, :]
        decay = jnp.exp(jnp.where(causal, seg, -jnp.inf))
        cb = jnp.einsum('bign,bjgn->bgij', c_q, b_q)
        w = cb[:, :, None] * decay * dt_t[..., None, :]
        y_intra = jnp.einsum('bgrij,bjgrp->bigrp', w, x_q)
        y_inter = jnp.einsum('bign,bgrpn->bigrp', c_q, state) * jnp.moveaxis(jnp.exp(cum), -1, 1)[..., None]
        to_end = jnp.exp(cum[..., -1:] - cum) * dt_t
        new_state = (state * jnp.exp(cum[..., -1])[..., None, None]
                     + jnp.einsum('bjgn,bgrj,bjgrp->bgrpn', b_q, to_end, x_q))
        return new_state, y_intra + y_inter

    state0 = jnp.zeros((bsz, SSM_GROUPS, HEADS_PER_GROUP, SSM_HEAD_DIM, D_STATE), jnp.float32)
    _, ys = lax.scan(step, state0, (xc, dtc, bc, cc))
    return jnp.moveaxis(ys, 0, 1).reshape(xh.shape)


def mamba2_branch(z, xbc, dt_raw, conv_w, conv_b, dt_bias, a_log, d_skip, ssm_norm_g):
    bsz, s, _ = z.shape
    xbc = jax.nn.silu(causal_depthwise_conv(xbc, conv_w, conv_b))
    xs, bm, cm = jnp.split(xbc, [D_INNER, D_INNER + SSM_GROUPS * D_STATE], axis=-1)
    xh = xs.astype(jnp.float32).reshape(bsz, s, SSM_GROUPS, HEADS_PER_GROUP, SSM_HEAD_DIM)
    bm = bm.astype(jnp.float32).reshape(bsz, s, SSM_GROUPS, D_STATE)
    cm = cm.astype(jnp.float32).reshape(bsz, s, SSM_GROUPS, D_STATE)
    dt = jax.nn.softplus(dt_raw.astype(jnp.float32) + dt_bias.astype(jnp.float32))
    dt = dt.reshape(bsz, s, SSM_GROUPS, HEADS_PER_GROUP)
    a = -jnp.exp(a_log.astype(jnp.float32)).reshape(SSM_GROUPS, HEADS_PER_GROUP)
    dsk = d_skip.astype(jnp.float32).reshape(SSM_GROUPS, HEADS_PER_GROUP)[:, :, None]
    y = ssd_chunked_scan(xh, dt, a, bm, cm) + dsk * xh
    y = y.reshape(bsz, s, D_INNER) * jax.nn.silu(z.astype(jnp.float32))
    yg = y.reshape(bsz, s, SSM_GROUPS, D_INNER // SSM_GROUPS)
    yg = yg * lax.rsqrt(jnp.mean(yg * yg, axis=-1, keepdims=True) + EPS)
    return (yg.reshape(bsz, s, D_INNER) * ssm_norm_g.astype(jnp.float32)).astype(z.dtype)


def multiscale_pool_branch(u, w_pool_grp, pool_scale):
    bsz, s, _ = u.shape
    uf = u.astype(jnp.float32).reshape(bsz, s, POOL_GROUPS, POOL_GROUP_DIM)
    cs = jnp.cumsum(uf, axis=1)
    pos = jnp.arange(1, s + 1, dtype=jnp.float32)
    outs = []
    for gi, win in enumerate(POOL_WINDOWS):
        c = cs[:, :, gi]
        prev = jnp.pad(c, ((0, 0), (win, 0), (0, 0)))[:, :s]
        mean = (c - prev) / jnp.minimum(pos, float(win))[None, :, None]
        outs.append(mean - uf[:, :, gi])
    pooled = jnp.stack(outs, axis=2)
    mixed = jnp.einsum('bsgc,gcd->bsgd', pooled, w_pool_grp.astype(jnp.float32))
    return (mixed.reshape(bsz, s, D_POOL) * pool_scale.astype(jnp.float32)).astype(u.dtype)


def setup_inputs(seed: int = 0) -> dict:
    key = jax.random.key(seed)
    ks = jax.random.split(key, 24)
    f32 = jnp.float32

    def nrm(k, shape, scale):
        return jax.random.normal(k, shape, f32) * scale

    def gain(k, shape):
        return 1.0 + 0.1 * jax.random.normal(k, shape, f32)

    dt0 = jnp.exp(jax.random.uniform(ks[5], (DEPTH, SSM_HEADS), f32, math.log(1e-3), math.log(1e-1)))
    dt_bias = dt0 + jnp.log(-jnp.expm1(-dt0))
    a_log = jnp.log(jax.random.uniform(ks[6], (DEPTH, SSM_HEADS), f32, 1.0, 16.0))
    return {
        "x": nrm(ks[0], (BATCH, SEQ, D_MODEL), 1.0),
        "p": nrm(ks[1], (DEPTH, BATCH, SEQ, PLE_DIM), 1.0),
        "w_in": nrm(ks[2], (DEPTH, D_MODEL, IN_PROJ_DIM), D_MODEL ** -0.5),
        "conv_w": nrm(ks[3], (DEPTH, CONV_WIDTH, CONV_DIM), CONV_WIDTH ** -0.5),
        "conv_b": nrm(ks[4], (DEPTH, CONV_DIM), 0.02),
        "dt_bias": dt_bias,
        "a_log": a_log,
        "d_skip": gain(ks[7], (DEPTH, SSM_HEADS)),
        "ssm_norm_g": gain(ks[8], (DEPTH, D_INNER)),
        "w_ssm_out": nrm(ks[9], (DEPTH, D_INNER, D_MODEL), D_INNER ** -0.5),
        "w_pool_grp": nrm(ks[10], (DEPTH, POOL_GROUPS, POOL_GROUP_DIM, POOL_GROUP_DIM), POOL_GROUP_DIM ** -0.5),
        "pool_scale": gain(ks[11], (DEPTH, D_POOL)),
        "w_pool_out": nrm(ks[12], (DEPTH, D_POOL, D_MODEL), D_POOL ** -0.5),
        "w_mix_out": nrm(ks[13], (DEPTH, D_MODEL, D_MODEL), D_MODEL ** -0.5),
        "g_pre_mix": gain(ks[14], (DEPTH, D_MODEL)),
        "g_post_mix": gain(ks[15], (DEPTH, D_MODEL)),
        "g_pre_ffn": gain(ks[16], (DEPTH, D_MODEL)),
        "g_post_ffn": gain(ks[17], (DEPTH, D_MODEL)),
        "w_ffn_in": nrm(ks[18], (DEPTH, D_MODEL, 2 * D_FF), D_MODEL ** -0.5),
        "w_ffn_out": nrm(ks[19], (DEPTH, D_FF, D_MODEL), D_FF ** -0.5),
        "w_ple_proj": nrm(ks[20], (DEPTH, PLE_DIM, D_MODEL), PLE_DIM ** -0.5),
        "w_ple_gate": nrm(ks[21], (DEPTH, D_MODEL, D_MODEL), D_MODEL ** -0.5),
    }


def reference(x, p, w_in, conv_w, conv_b, dt_bias, a_log, d_skip, ssm_norm_g, w_ssm_out,
              w_pool_grp, pool_scale, w_pool_out, w_mix_out, g_pre_mix, g_post_mix,
              g_pre_ffn, g_post_ffn, w_ffn_in, w_ffn_out, w_ple_proj, w_ple_gate):
    splits = list(np.cumsum([D_INNER, CONV_DIM, SSM_HEADS, D_POOL, D_MODEL]))
    for i in range(DEPTH):
        h = rmsnorm(x, g_pre_mix[i])
        proj = jnp.einsum('bsd,de->bse', h, w_in[i])
        z, xbc, dt_raw, u_pool, gate_ssm, gate_pool = jnp.split(proj, splits, axis=-1)
        y_ssm = mamba2_branch(z, xbc, dt_raw, conv_w[i], conv_b[i], dt_bias[i], a_log[i],
                              d_skip[i], ssm_norm_g[i])
        y_pool = multiscale_pool_branch(u_pool, w_pool_grp[i], pool_scale[i])
        merged = (jax.nn.sigmoid(gate_ssm) * jnp.einsum('bse,ed->bsd', y_ssm, w_ssm_out[i])
                  + jax.nn.sigmoid(gate_pool) * jnp.einsum('bse,ed->bsd', y_pool, w_pool_out[i]))
        mix = jnp.einsum('bsd,de->bse', merged, w_mix_out[i])
        x = x + rmsnorm(mix, g_post_mix[i])
        h2 = rmsnorm(x, g_pre_ffn[i])
        gu = jnp.einsum('bsd,df->bsf', h2, w_ffn_in[i])
        g_ff, u_ff = jnp.split(gu, 2, axis=-1)
        f = jnp.einsum('bsf,fd->bsd', jax.nn.silu(g_ff) * u_ff, w_ffn_out[i])
        x = x + rmsnorm(f, g_post_ffn[i])
        ple = jnp.einsum('bsk,kd->bsd', p[i], w_ple_proj[i])
        x = x + jax.nn.sigmoid(jnp.einsum('bsd,de->bse', x, w_ple_gate[i])) * ple
    return x
```

```python
import functools

import jax
import jax.numpy as jnp
import numpy as np
from jax import lax
from jax.experimental import pallas as pl
from jax.experimental.pallas import tpu as pltpu

F32 = jnp.float32
BF16 = jnp.bfloat16

EPS = 1e-6
SSM_HEAD_DIM = 64
SSM_GROUPS = 8
D_STATE = 128
CONV_WIDTH = 4
POOL_WINDOWS = (2, 4, 8, 16)
POOL_HALO = 16
SSD_Q = 128

MIB = 1024 * 1024


def _cparams(semantics, vmem_mib):
    return pltpu.CompilerParams(dimension_semantics=semantics,
                                vmem_limit_bytes=vmem_mib * MIB)


def _rms(v, g):
    ms = jnp.mean(v * v, axis=-1, keepdims=True)
    return v * lax.rsqrt(ms + EPS) * g


def _silu(v):
    return v * jax.nn.sigmoid(v)


def _softplus(v):
    return jnp.maximum(v, 0.0) + jnp.log1p(jnp.exp(-jnp.abs(v)))


def _split_terms(v, n_terms):
    pieces = []
    rem = v
    for _ in range(n_terms - 1):
        hi = rem.astype(BF16).astype(F32)
        pieces.append(hi)
        rem = rem - hi
    pieces.append(rem)
    return pieces


def _in_proj_kernel(x_ref, g_ref, w_ref, wdt_ref, o_ref, dt_ref, h_scr, *, n_chunks):
    @pl.when(pl.program_id(1) == 0)
    def _():
        def body(q, carry):
            rows = pl.ds(pl.multiple_of(q * SSD_Q, SSD_Q), SSD_Q)
            h = _rms(x_ref[rows, :], g_ref[...]).astype(BF16)
            h_scr[rows, :] = h
            dt_ref[q] = lax.dot_general(wdt_ref[...], h, (((1,), (1,)), ((), ())),
                                        preferred_element_type=F32)
            return carry
        lax.fori_loop(0, n_chunks, body, 0)

    o_ref[...] = jnp.dot(h_scr[...], w_ref[...],
                         preferred_element_type=F32).astype(o_ref.dtype)


def _in_proj(x2d, g, w_main, w_dt_t, *, tm=1024, tn=1024):
    t, d = x2d.shape
    n = w_main.shape[1]
    heads = w_dt_t.shape[0]
    n_chunks = tm // SSD_Q
    return pl.pallas_call(
        functools.partial(_in_proj_kernel, n_chunks=n_chunks),
        grid=(t // tm, n // tn),
        in_specs=[
            pl.BlockSpec((tm, d), lambda i, j: (i, 0)),
            pl.BlockSpec((1, d), lambda i, j: (0, 0)),
            pl.BlockSpec((d, tn), lambda i, j: (0, j)),
            pl.BlockSpec((heads, d), lambda i, j: (0, 0)),
        ],
        out_specs=[
            pl.BlockSpec((tm, tn), lambda i, j: (i, j)),
            pl.BlockSpec((n_chunks, heads, SSD_Q), lambda i, j: (i, 0, 0)),
        ],
        out_shape=[
            jax.ShapeDtypeStruct((t, n), BF16),
            jax.ShapeDtypeStruct((t // SSD_Q, heads, SSD_Q), F32),
        ],
        scratch_shapes=[pltpu.VMEM((tm, d), BF16)],
        compiler_params=_cparams(("parallel", "arbitrary"), 48),
        name="in_proj",
    )(x2d, g, w_main, w_dt_t)


def _ssd_kernel(z_ref, xs_ref, b_ref, c_ref, dt_ref, cw_ref, cb_ref, dtb_ref, alog_ref,
                dsk_ref, ng_ref, u3_ref, eb_ref, eq_ref, y_ref,
                state_scr, ubuf, xc, *, lc, gw):
    q_len = SSD_Q
    n_chunks = lc // q_len
    hpg = gw // SSM_HEAD_DIM
    halo = 8
    cwid = gw + 2 * D_STATE

    @pl.when(pl.program_id(2) == 0)
    def _():
        state_scr[...] = jnp.zeros_like(state_scr)
        ubuf[0:halo, :] = jnp.zeros((halo, cwid), F32)

    ubuf[halo:halo + lc, 0:gw] = xs_ref[...].astype(F32)
    ubuf[halo:halo + lc, gw:gw + D_STATE] = b_ref[...].astype(F32)
    ubuf[halo:halo + lc, gw + D_STATE:cwid] = c_ref[...].astype(F32)

    cw = cw_ref[...]
    cbias = cb_ref[...]
    for q in range(n_chunks):
        acc = cbias
        for k in range(CONV_WIDTH):
            off = halo + q * q_len - (CONV_WIDTH - 1) + k
            acc = acc + cw[k:k + 1, :] * ubuf[off:off + q_len, :]
        xc[q * q_len:(q + 1) * q_len, :] = _silu(acc)
    ubuf[0:halo, :] = ubuf[lc:lc + halo, :]

    dtb = dtb_ref[...]
    a_col = -jnp.exp(alog_ref[...])
    dsk = dsk_ref[...]
    ng = ng_ref[...]
    row_i = lax.broadcasted_iota(jnp.int32, (q_len, q_len), 0)
    col_j = lax.broadcasted_iota(jnp.int32, (q_len, q_len), 1)
    causal = row_i >= col_j

    def chunk(k, carry):
        rows = pl.ds(pl.multiple_of(k * q_len, q_len), q_len)
        xq = xc[rows, 0:gw]
        bq = xc[rows, gw:gw + D_STATE].astype(BF16)
        cq = xc[rows, gw + D_STATE:cwid].astype(BF16)

        dt_row = _softplus(dt_ref[k] + dtb)
        da = dt_row * a_col
        da3 = jnp.concatenate(_split_terms(da, 3), axis=1).astype(BF16)
        cum_row = jnp.dot(da3, u3_ref[...], preferred_element_type=F32)
        cum_last = cum_row[:, q_len - 1:q_len]
        e_row = jnp.exp(cum_row)
        te_row = jnp.exp(cum_last - cum_row) * dt_row

        v = jnp.concatenate([e_row, dt_row, te_row], axis=0)
        vs = jnp.concatenate(_split_terms(v, 2), axis=0).astype(BF16)
        x3 = lax.dot_general(vs, eb_ref[...], (((0,), (0,)), ((), ())),
                             preferred_element_type=F32)
        ec_exp = x3[:, 0:gw]
        dt_exp = x3[:, gw:2 * gw]
        te_exp = x3[:, 2 * gw:3 * gw]
        c3 = jnp.concatenate(_split_terms(cum_row, 3), axis=0).astype(BF16)
        cum_i = lax.dot_general(c3, eq_ref[...], (((0,), (0,)), ((), ())),
                                preferred_element_type=F32)

        xdt = (xq * dt_exp).astype(BF16)
        xte = (xq * te_exp).astype(BF16)
        cb = lax.dot_general(cq, bq, (((1,), (1,)), ((), ())),
                             preferred_element_type=F32)
        ys = []
        for r in range(hpg):
            seg = cum_i[:, r * q_len:(r + 1) * q_len] - cum_row[r:r + 1, :]
            decay = jnp.exp(jnp.where(causal, seg, -jnp.inf))
            w = (cb * decay).astype(BF16)
            ys.append(jnp.dot(w, xdt[:, r * SSM_HEAD_DIM:(r + 1) * SSM_HEAD_DIM],
                              preferred_element_type=F32))
        y_intra = jnp.concatenate(ys, axis=1)

        st = state_scr[...]
        y_inter = jnp.dot(cq, st.astype(BF16), preferred_element_type=F32) * ec_exp
        st_new = (st * ec_exp[q_len - 1:q_len, :]
                  + lax.dot_general(bq, xte, (((0,), (0,)), ((), ())),
                                    preferred_element_type=F32))
        state_scr[...] = st_new

        zq = z_ref[rows, :].astype(F32)
        y = (y_intra + y_inter + dsk * xq) * _silu(zq)
        y_ref[rows, :] = _rms(y, ng).astype(y_ref.dtype)
        return carry

    lax.fori_loop(0, n_chunks, chunk, 0)


def _ssd(proj, dt_t, cw, cb, dtb, alog, dsk, ng, *, batch, seq, d_inner, lc=1024):
    t = proj.shape[0]
    groups = SSM_GROUPS
    gw = d_inner // groups
    hpg = gw // SSM_HEAD_DIM
    q_len = SSD_Q
    n_tiles = seq // lc
    cwid = gw + 2 * D_STATE
    xs_blk0 = d_inner // gw
    b_blk0 = 2 * d_inner // D_STATE
    c_blk0 = b_blk0 + groups

    tri = np.triu(np.ones((q_len, q_len), np.float32))
    u3 = jnp.asarray(np.concatenate([tri, tri, tri], axis=0), BF16)
    rep = np.kron(np.eye(hpg, dtype=np.float32), np.ones((1, SSM_HEAD_DIM), np.float32))
    eb = np.kron(np.eye(3, dtype=np.float32), rep)
    eb = jnp.asarray(np.concatenate([eb, eb], axis=0), BF16)
    rq = np.kron(np.eye(hpg, dtype=np.float32), np.ones((1, q_len), np.float32))
    eq = jnp.asarray(np.concatenate([rq, rq, rq], axis=0), BF16)

    row = lambda b, g, c: b * n_tiles + c
    const2 = lambda b, g, c: (0, 0)
    per_group = lambda b, g, c: (g, 0, 0)
    return pl.pallas_call(
        functools.partial(_ssd_kernel, lc=lc, gw=gw),
        grid=(batch, groups, n_tiles),
        in_specs=[
            pl.BlockSpec((lc, gw), lambda b, g, c: (row(b, g, c), g)),
            pl.BlockSpec((lc, gw), lambda b, g, c: (row(b, g, c), xs_blk0 + g)),
            pl.BlockSpec((lc, D_STATE), lambda b, g, c: (row(b, g, c), b_blk0 + g)),
            pl.BlockSpec((lc, D_STATE), lambda b, g, c: (row(b, g, c), c_blk0 + g)),
            pl.BlockSpec((lc // q_len, hpg, q_len), lambda b, g, c: (row(b, g, c), g, 0)),
            pl.BlockSpec((None, CONV_WIDTH, cwid), per_group),
            pl.BlockSpec((None, 1, cwid), per_group),
            pl.BlockSpec((None, hpg, 1), per_group),
            pl.BlockSpec((None, hpg, 1), per_group),
            pl.BlockSpec((None, 1, gw), per_group),
            pl.BlockSpec((None, 1, gw), per_group),
            pl.BlockSpec(u3.shape, const2),
            pl.BlockSpec(eb.shape, const2),
            pl.BlockSpec(eq.shape, const2),
        ],
        out_specs=pl.BlockSpec((lc, gw), lambda b, g, c: (row(b, g, c), g)),
        out_shape=jax.ShapeDtypeStruct((t, d_inner), BF16),
        scratch_shapes=[
            pltpu.VMEM((D_STATE, gw), F32),
            pltpu.VMEM((lc + 8, cwid), F32),
            pltpu.VMEM((lc, cwid), F32),
        ],
        compiler_params=_cparams(("parallel", "parallel", "arbitrary"), 32),
        name="ssd",
    )(proj, proj, proj, proj, dt_t, cw, cb, dtb, alog, dsk, ng, u3, eb, eq)


def _pool_kernel(u_ref, halo_ref, wg_ref, sc_ref, o_ref, buf, *, tp, tiles_per_seq, gdim):
    i = pl.program_id(0)
    seq_tile = i % tiles_per_seq
    keep = (seq_tile != 0).astype(F32)
    buf[0:POOL_HALO, :] = halo_ref[...].astype(F32) * keep
    buf[POOL_HALO:POOL_HALO + tp, :] = u_ref[...].astype(F32)

    rc = 128
    for r0 in range(0, tp, rc):
        pos = (seq_tile * tp + r0 + 1
               + lax.broadcasted_iota(jnp.int32, (rc, 1), 0)).astype(F32)
        for gi, win in enumerate(POOL_WINDOWS):
            cols = slice(gi * gdim, (gi + 1) * gdim)
            base = POOL_HALO + r0
            cur = buf[base:base + rc, cols]
            s = cur
            for k in range(1, win):
                s = s + buf[base - k:base - k + rc, cols]
            pooled = (s / jnp.minimum(pos, float(win)) - cur).astype(BF16)
            mixed = jnp.dot(pooled, wg_ref[gi], preferred_element_type=F32)
            o_ref[r0:r0 + rc, cols] = (mixed * sc_ref[:, cols]).astype(o_ref.dtype)


def _pool(proj, w_grp, scale, *, seq, d_pool, col0, tp=512):
    t = proj.shape[0]
    n_groups = len(POOL_WINDOWS)
    gdim = d_pool // n_groups
    tiles_per_seq = seq // tp
    cblk = col0 // d_pool
    hb = tp // POOL_HALO
    return pl.pallas_call(
        functools.partial(_pool_kernel, tp=tp, tiles_per_seq=tiles_per_seq, gdim=gdim),
        grid=(t // tp,),
        in_specs=[
            pl.BlockSpec((tp, d_pool), lambda i: (i, cblk)),
            pl.BlockSpec((POOL_HALO, d_pool), lambda i: (jnp.maximum(i * hb - 1, 0), cblk)),
            pl.BlockSpec((n_groups, gdim, gdim), lambda i: (0, 0, 0)),
            pl.BlockSpec((1, d_pool), lambda i: (0, 0)),
        ],
        out_specs=pl.BlockSpec((tp, d_pool), lambda i: (i, 0)),
        out_shape=jax.ShapeDtypeStruct((t, d_pool), BF16),
        scratch_shapes=[pltpu.VMEM((tp + POOL_HALO, d_pool), F32)],
        compiler_params=_cparams(("parallel",), 32),
        name="pool",
    )(proj, proj, w_grp, scale)


def _merge_kernel(ys_ref, yp_ref, gs_ref, gp_ref, ws_ref, wp_ref, o_ref):
    a = jnp.dot(ys_ref[...], ws_ref[...], preferred_element_type=F32)
    b = jnp.dot(yp_ref[...], wp_ref[...], preferred_element_type=F32)
    merged = (jax.nn.sigmoid(gs_ref[...].astype(F32)) * a
              + jax.nn.sigmoid(gp_ref[...].astype(F32)) * b)
    o_ref[...] = merged.astype(o_ref.dtype)


def _merge(y_ssm, y_pool, proj, w_so, w_po, *, gs_col0, gp_col0, tm=1024, tn=512):
    t, d_inner = y_ssm.shape
    d_pool = y_pool.shape[1]
    d = w_so.shape[1]
    gs_blk = gs_col0 // tn
    gp_blk = gp_col0 // tn
    return pl.pallas_call(
        _merge_kernel,
        grid=(t // tm, d // tn),
        in_specs=[
            pl.BlockSpec((tm, d_inner), lambda i, j: (i, 0)),
            pl.BlockSpec((tm, d_pool), lambda i, j: (i, 0)),
            pl.BlockSpec((tm, tn), lambda i, j: (i, gs_blk + j)),
            pl.BlockSpec((tm, tn), lambda i, j: (i, gp_blk + j)),
            pl.BlockSpec((d_inner, tn), lambda i, j: (0, j)),
            pl.BlockSpec((d_pool, tn), lambda i, j: (0, j)),
        ],
        out_specs=pl.BlockSpec((tm, tn), lambda i, j: (i, j)),
        out_shape=jax.ShapeDtypeStruct((t, d), BF16),
        compiler_params=_cparams(("parallel", "arbitrary"), 48),
        name="merge",
    )(y_ssm, y_pool, proj, proj, w_so, w_po)


def _mix_kernel(m_ref, x_ref, w_ref, gpost_ref, gpre_ref, x1_ref, h2_ref):
    mix = jnp.dot(m_ref[...], w_ref[...], preferred_element_type=F32)
    x1 = x_ref[...] + _rms(mix, gpost_ref[...])
    x1_ref[...] = x1
    h2_ref[...] = _rms(x1, gpre_ref[...]).astype(h2_ref.dtype)


def _mix(merged, x2d, w_mix, g_post, g_pre, *, tm=512):
    t, d = x2d.shape
    return pl.pallas_call(
        _mix_kernel,
        grid=(t // tm,),
        in_specs=[
            pl.BlockSpec((tm, d), lambda i: (i, 0)),
            pl.BlockSpec((tm, d), lambda i: (i, 0)),
            pl.BlockSpec((d, d), lambda i: (0, 0)),
            pl.BlockSpec((1, d), lambda i: (0, 0)),
            pl.BlockSpec((1, d), lambda i: (0, 0)),
        ],
        out_specs=[
            pl.BlockSpec((tm, d), lambda i: (i, 0)),
            pl.BlockSpec((tm, d), lambda i: (i, 0)),
        ],
        out_shape=[
            jax.ShapeDtypeStruct((t, d), F32),
            jax.ShapeDtypeStruct((t, d), BF16),
        ],
        compiler_params=_cparams(("parallel",), 48),
        name="mix",
    )(merged, x2d, w_mix, g_post, g_pre)


def _ffn_kernel(h_ref, x1_ref, wg_ref, wu_ref, wo_ref, g_ref, o_ref):
    j = pl.program_id(1)
    h = h_ref[...]
    gate = jnp.dot(h, wg_ref[...], preferred_element_type=F32)
    up = jnp.dot(h, wu_ref[...], preferred_element_type=F32)
    act = (_silu(gate) * up).astype(BF16)
    part = jnp.dot(act, wo_ref[...], preferred_element_type=F32)

    @pl.when(j == 0)
    def _():
        o_ref[...] = part

    @pl.when(j != 0)
    def _():
        o_ref[...] += part

    @pl.when(j == pl.num_programs(1) - 1)
    def _():
        o_ref[...] = x1_ref[...] + _rms(o_ref[...], g_ref[...])


def _ffn(h2, x1, w_ffn_in, w_out, g_post, *, tm=512, tf=512):
    t, d = x1.shape
    d_ff = w_out.shape[0]
    n_f = d_ff // tf
    return pl.pallas_call(
        _ffn_kernel,
        grid=(t // tm, n_f),
        in_specs=[
            pl.BlockSpec((tm, d), lambda i, j: (i, 0)),
            pl.BlockSpec((tm, d), lambda i, j: (i, 0)),
            pl.BlockSpec((d, tf), lambda i, j: (0, j)),
            pl.BlockSpec((d, tf), lambda i, j: (0, n_f + j)),
            pl.BlockSpec((tf, d), lambda i, j: (j, 0)),
            pl.BlockSpec((1, d), lambda i, j: (0, 0)),
        ],
        out_specs=pl.BlockSpec((tm, d), lambda i, j: (i, 0)),
        out_shape=jax.ShapeDtypeStruct((t, d), F32),
        compiler_params=_cparams(("parallel", "arbitrary"), 48),
        name="ffn",
    )(h2, x1, w_ffn_in, w_ffn_in, w_out, g_post)


def _ple_kernel(x_ref, p_ref, wg_ref, wp_ref, o_ref):
    x = x_ref[...]
    gate = jnp.dot(x.astype(BF16), wg_ref[...], preferred_element_type=F32)
    ple = jnp.dot(p_ref[...].astype(BF16), wp_ref[...], preferred_element_type=F32)
    o_ref[...] = x + jax.nn.sigmoid(gate) * ple


def _ple(x2, p2d, w_gate, w_proj, *, tm=512):
    t, d = x2.shape
    k = p2d.shape[1]
    return pl.pallas_call(
        _ple_kernel,
        grid=(t // tm,),
        in_specs=[
            pl.BlockSpec((tm, d), lambda i: (i, 0)),
            pl.BlockSpec((tm, k), lambda i: (i, 0)),
            pl.BlockSpec((d, d), lambda i: (0, 0)),
            pl.BlockSpec((k, d), lambda i: (0, 0)),
        ],
        out_specs=pl.BlockSpec((tm, d), lambda i: (i, 0)),
        out_shape=jax.ShapeDtypeStruct((t, d), F32),
        compiler_params=_cparams(("parallel",), 48),
        name="ple",
    )(x2, p2d, w_gate, w_proj)


def kernel(x, p, w_in, conv_w, conv_b, dt_bias, a_log, d_skip, ssm_norm_g, w_ssm_out,
           w_pool_grp, pool_scale, w_pool_out, w_mix_out, g_pre_mix, g_post_mix,
           g_pre_ffn, g_post_ffn, w_ffn_in, w_ffn_out, w_ple_proj, w_ple_gate):
    batch, seq, d = x.shape
    depth = w_in.shape[0]
    d_inner = w_ssm_out.shape[1]
    d_pool = w_pool_out.shape[1]
    heads = dt_bias.shape[1]
    groups = SSM_GROUPS
    hpg = heads // groups
    gw = d_inner // groups
    n_bc = groups * D_STATE
    conv_dim = d_inner + 2 * n_bc
    t = batch * seq
    assert heads * SSM_HEAD_DIM == d_inner and conv_w.shape[2] == conv_dim
    assert w_in.shape[2] == d_inner + conv_dim + heads + d_pool + 2 * d

    dt_col0 = d_inner + conv_dim
    pool_col0 = dt_col0
    gs_col0 = pool_col0 + d_pool
    gp_col0 = gs_col0 + d

    x2d = x.reshape(t, d)
    for li in range(depth):
        wi = w_in[li]
        w_main = jnp.concatenate([wi[:, :dt_col0], wi[:, dt_col0 + heads:]], axis=1).astype(BF16)
        w_dt_t = wi[:, dt_col0:dt_col0 + heads].T.astype(BF16)

        def per_group_cols(v):
            xs = v[..., :d_inner].reshape(v.shape[:-1] + (groups, gw))
            bm = v[..., d_inner:d_inner + n_bc].reshape(v.shape[:-1] + (groups, D_STATE))
            cm = v[..., d_inner + n_bc:].reshape(v.shape[:-1] + (groups, D_STATE))
            cat = jnp.concatenate([xs, bm, cm], axis=-1)
            return jnp.moveaxis(cat, -2, 0)
        cw = per_group_cols(conv_w[li])
        cb = per_group_cols(conv_b[li][None, :])
        dtb = dt_bias[li].reshape(groups, hpg, 1)
        alog = a_log[li].reshape(groups, hpg, 1)
        dsk = jnp.repeat(d_skip[li], SSM_HEAD_DIM).reshape(groups, 1, gw)
        ng = ssm_norm_g[li].reshape(groups, 1, gw)

        proj, dt_t = _in_proj(x2d, g_pre_mix[li][None, :], w_main, w_dt_t)
        y_ssm = _ssd(proj, dt_t, cw, cb, dtb, alog, dsk, ng,
                     batch=batch, seq=seq, d_inner=d_inner)
        y_pool = _pool(proj, w_pool_grp[li].astype(BF16), pool_scale[li][None, :],
                       seq=seq, d_pool=d_pool, col0=pool_col0)
        merged = _merge(y_ssm, y_pool, proj, w_ssm_out[li].astype(BF16),
                        w_pool_out[li].astype(BF16), gs_col0=gs_col0, gp_col0=gp_col0)
        x1, h2 = _mix(merged, x2d, w_mix_out[li].astype(BF16),
                      g_post_mix[li][None, :], g_pre_ffn[li][None, :])
        x2 = _ffn(h2, x1, w_ffn_in[li].astype(BF16), w_ffn_out[li].astype(BF16),
                  g_post_ffn[li][None, :])
        x2d = _ple(x2, p[li].reshape(t, -1), w_ple_gate[li].astype(BF16),
                   w_ple_proj[li].astype(BF16))
    return x2d.reshape(batch, seq, d)
```

```python
import functools

import jax
import jax.numpy as jnp
import numpy as np
from jax import lax
from jax.experimental import pallas as pl
from jax.experimental.pallas import tpu as pltpu

F32 = jnp.float32
BF16 = jnp.bfloat16

EPS = 1e-6
SSM_HEAD_DIM = 64
SSM_GROUPS = 8
D_STATE = 128
CONV_WIDTH = 4
POOL_WINDOWS = (2, 4, 8, 16)
POOL_HALO = 16
SSD_Q = 128
LANES = 128
CONV_PHASES = 4

LOG2_E = 1.4426950408889634

MIB = 1024 * 1024


def _cparams(semantics, vmem_mib):
    return pltpu.CompilerParams(dimension_semantics=semantics,
                                vmem_limit_bytes=vmem_mib * MIB)


def _rms(v, g):
    ms = jnp.mean(v * v, axis=-1, keepdims=True)
    return v * lax.rsqrt(ms + EPS) * g


def _silu(v):
    return v * jax.nn.sigmoid(v)


def _softplus(v):
    return jnp.maximum(v, 0.0) + jnp.log1p(jnp.exp(-jnp.abs(v)))


def _split_terms(v, n_terms):
    pieces = []
    rem = v
    for _ in range(n_terms - 1):
        hi = rem.astype(BF16).astype(F32)
        pieces.append(hi)
        rem = rem - hi
    pieces.append(rem)
    return pieces


def _in_proj_kernel(x_ref, g_ref, w_ref, wdt_ref, o_ref, dt_ref, h_scr, *, n_chunks):
    @pl.when(pl.program_id(1) == 0)
    def _():
        def body(q, carry):
            rows = pl.ds(pl.multiple_of(q * SSD_Q, SSD_Q), SSD_Q)
            h = _rms(x_ref[rows, :], g_ref[...]).astype(BF16)
            h_scr[rows, :] = h
            dt_ref[q] = lax.dot_general(wdt_ref[...], h, (((1,), (1,)), ((), ())),
                                        preferred_element_type=F32)
            return carry
        lax.fori_loop(0, n_chunks, body, 0)

    o_ref[...] = jnp.dot(h_scr[...], w_ref[...],
                         preferred_element_type=F32).astype(o_ref.dtype)


def _in_proj(x2d, g, w_main, w_dt_t, *, tm=1024, tn=1024):
    t, d = x2d.shape
    n = w_main.shape[1]
    heads = w_dt_t.shape[0]
    n_chunks = tm // SSD_Q
    return pl.pallas_call(
        functools.partial(_in_proj_kernel, n_chunks=n_chunks),
        grid=(t // tm, n // tn),
        in_specs=[
            pl.BlockSpec((tm, d), lambda i, j: (i, 0)),
            pl.BlockSpec((1, d), lambda i, j: (0, 0)),
            pl.BlockSpec((d, tn), lambda i, j: (0, j)),
            pl.BlockSpec((heads, d), lambda i, j: (0, 0)),
        ],
        out_specs=[
            pl.BlockSpec((tm, tn), lambda i, j: (i, j)),
            pl.BlockSpec((n_chunks, heads, SSD_Q), lambda i, j: (i, 0, 0)),
        ],
        out_shape=[
            jax.ShapeDtypeStruct((t, n), BF16),
            jax.ShapeDtypeStruct((t // SSD_Q, heads, SSD_Q), F32),
        ],
        scratch_shapes=[pltpu.VMEM((tm, d), BF16)],
        compiler_params=_cparams(("parallel", "arbitrary"), 48),
        name="in_proj",
    )(x2d, g, w_main, w_dt_t)


def _ssd_kernel(z_ref, xs_ref, b_ref, c_ref, dt_ref, cw_ref, cb_ref, dtb_ref, alog_ref,
                dsk_ref, ng_ref, u3_ref, ex_ref, rep_ref, y_ref,
                state_scr, ubuf, xc, lhs_scr, cumj_scr, dl_scr, ci_scr, g_scr, s_scr,
                *, lc, gw):
    q_len = SSD_Q
    n_chunks = lc // q_len
    hpg = gw // SSM_HEAD_DIM
    ns_x = gw // LANES
    ns = ns_x + 2
    halo = 8
    pr = q_len // CONV_PHASES

    @pl.when(pl.program_id(2) == 0)
    def _():
        state_scr[...] = jnp.zeros_like(state_scr)
        ubuf[:, 0:halo, :] = jnp.zeros((ns, halo, LANES), F32)

    for s in range(ns_x):
        ubuf[s, halo:halo + lc, :] = xs_ref[:, s * LANES:(s + 1) * LANES].astype(F32)
    ubuf[ns_x, halo:halo + lc, :] = b_ref[...].astype(F32)
    ubuf[ns_x + 1, halo:halo + lc, :] = c_ref[...].astype(F32)

    dt_all = _softplus(dt_ref[...] + dtb_ref[...][None]).reshape(n_chunks * hpg, q_len)
    a_all = jnp.broadcast_to(-jnp.exp(alog_ref[...])[None],
                             (n_chunks, hpg, 1)).reshape(n_chunks * hpg, 1)
    da3 = jnp.concatenate(_split_terms(dt_all * a_all, 3), axis=1).astype(BF16)
    cum2 = jnp.dot(da3, u3_ref[...], preferred_element_type=F32) * LOG2_E
    cum2_last = cum2[:, q_len - 1:q_len]
    te_all = jnp.exp2(cum2_last - cum2) * dt_all
    cumj_scr[...] = (cum2 - jnp.log2(dt_all)).reshape(n_chunks, hpg, q_len)
    pieces = [p.reshape(n_chunks, hpg, q_len)
              for p in _split_terms(cum2, 3) + _split_terms(te_all, 2)]
    pieces.append(jnp.zeros((n_chunks, hpg, q_len), F32))
    lhs_scr[...] = jnp.concatenate(pieces, axis=1).astype(BF16)
    dl_scr[...] = jnp.sum(jnp.exp2(cum2_last).reshape(n_chunks, hpg, 1) * rep_ref[...][None],
                          axis=1)

    def prep(k, carry):
        r0 = pl.multiple_of(k * q_len, q_len)
        rows = pl.ds(r0, q_len)

        for s in range(ns):
            lanes = slice(s * LANES, (s + 1) * LANES)
            taps = {d: ubuf.at[s][pl.ds(r0 + halo + d, pr, stride=CONV_PHASES), :]
                    for d in range(1 - CONV_WIDTH, CONV_PHASES)}
            for ph in range(CONV_PHASES):
                acc = cb_ref[:, lanes]
                for kk in range(CONV_WIDTH):
                    acc = acc + cw_ref[kk:kk + 1, lanes] * taps[ph - (CONV_WIDTH - 1) + kk]
                xc.at[s][pl.ds(r0 + ph, pr, stride=CONV_PHASES), :] = _silu(acc)

        xq = jnp.concatenate([xc[s, rows, :] for s in range(ns_x)], axis=1)
        bq = xc[ns_x, rows, :]
        cq = xc[ns_x + 1, rows, :]
        bq16 = bq.astype(BF16)
        cq16 = cq.astype(BF16)

        xp = lax.dot_general(lhs_scr[k], ex_ref[...], (((0,), (0,)), ((), ())),
                             preferred_element_type=F32)
        ci_scr[rows, :] = xp[:, 0:hpg * q_len]
        te_exp = xp[:, hpg * q_len:hpg * q_len + gw]

        cb = lax.dot_general(cq16, bq16, (((1,), (1,)), ((), ())),
                             preferred_element_type=F32)
        g_scr[rows, :] = jnp.concatenate([cb, cq], axis=1)

        xte = (xq * te_exp).astype(BF16)
        s_scr[k] = lax.dot_general(bq16, xte, (((0,), (0,)), ((), ())),
                                   preferred_element_type=F32)
        return carry

    lax.fori_loop(0, n_chunks, prep, 0, unroll=2)
    ubuf[:, 0:halo, :] = ubuf[:, lc:lc + halo, :]

    dsk = dsk_ref[...]
    ng = ng_ref[...]
    row_i = lax.broadcasted_iota(jnp.int32, (q_len, q_len), 0)
    col_j = lax.broadcasted_iota(jnp.int32, (q_len, q_len), 1)
    causal = row_i >= col_j
    low_half = lax.broadcasted_iota(jnp.int32, (1, LANES), 1) < SSM_HEAD_DIM

    def heads(k, carry):
        rows = pl.ds(pl.multiple_of(k * q_len, q_len), q_len)
        xq = jnp.concatenate([xc[s, rows, :] for s in range(ns_x)], axis=1)
        g_mat = g_scr[rows, :]
        cum_j = cumj_scr[k]
        st = state_scr[...]
        rhs = jnp.concatenate([xq.astype(BF16), st.astype(BF16)], axis=0)
        ys = []
        for m in range(hpg // 2):
            rp = rhs[:, m * LANES:(m + 1) * LANES]
            zero = jnp.zeros_like(rp)
            acc = None
            for r, rr in ((2 * m, jnp.where(low_half, rp, zero)),
                          (2 * m + 1, jnp.where(low_half, zero, rp))):
                ci = ci_scr[rows, r * q_len:(r + 1) * q_len]
                seg = jnp.where(causal, ci - cum_j[r:r + 1, :], -jnp.inf)
                lhs = (g_mat * jnp.exp2(jnp.concatenate([seg, ci], axis=1))).astype(BF16)
                part = jnp.dot(lhs, rr, preferred_element_type=F32)
                acc = part if acc is None else acc + part
            ys.append(acc)
        y_mm = jnp.concatenate(ys, axis=1)

        state_scr[...] = st * dl_scr[pl.ds(k, 1), :] + s_scr[k]

        zq = z_ref[rows, :].astype(F32)
        y = (y_mm + dsk * xq) * _silu(zq)
        y_ref[rows, :] = _rms(y, ng).astype(y_ref.dtype)
        return carry

    lax.fori_loop(0, n_chunks, heads, 0, unroll=2)


def _ssd(proj, dt_t, cw, cb, dtb, alog, dsk, ng, *, batch, seq, d_inner, lc=1024):
    t = proj.shape[0]
    groups = SSM_GROUPS
    gw = d_inner // groups
    hpg = gw // SSM_HEAD_DIM
    q_len = SSD_Q
    n_tiles = seq // lc
    n_chunks = lc // q_len
    cwid = gw + 2 * D_STATE
    ns = cwid // LANES
    assert D_STATE == q_len == LANES and hpg % 2 == 0 and gw % LANES == 0
    xs_blk0 = d_inner // gw
    b_blk0 = 2 * d_inner // D_STATE
    c_blk0 = b_blk0 + groups

    tri = np.triu(np.ones((q_len, q_len), np.float32))
    u3 = jnp.asarray(np.concatenate([tri, tri, tri], axis=0), BF16)
    rep = np.kron(np.eye(hpg, dtype=np.float32), np.ones((1, SSM_HEAD_DIM), np.float32))
    rq = np.kron(np.eye(hpg, dtype=np.float32), np.ones((1, q_len), np.float32))
    zq = np.zeros_like(rq)
    zr = np.zeros_like(rep)
    ex = np.concatenate([
        np.concatenate([rq, zr], axis=1), np.concatenate([rq, zr], axis=1),
        np.concatenate([rq, zr], axis=1),
        np.concatenate([zq, rep], axis=1), np.concatenate([zq, rep], axis=1),
        np.concatenate([zq, zr], axis=1),
    ], axis=0)
    ex = jnp.asarray(ex, BF16)
    rep = jnp.asarray(rep, F32)

    row = lambda b, g, c: b * n_tiles + c
    const2 = lambda b, g, c: (0, 0)
    per_group = lambda b, g, c: (g, 0, 0)
    return pl.pallas_call(
        functools.partial(_ssd_kernel, lc=lc, gw=gw),
        grid=(batch, groups, n_tiles),
        in_specs=[
            pl.BlockSpec((lc, gw), lambda b, g, c: (row(b, g, c), g)),
            pl.BlockSpec((lc, gw), lambda b, g, c: (row(b, g, c), xs_blk0 + g)),
            pl.BlockSpec((lc, D_STATE), lambda b, g, c: (row(b, g, c), b_blk0 + g)),
            pl.BlockSpec((lc, D_STATE), lambda b, g, c: (row(b, g, c), c_blk0 + g)),
            pl.BlockSpec((lc // q_len, hpg, q_len), lambda b, g, c: (row(b, g, c), g, 0)),
            pl.BlockSpec((None, CONV_WIDTH, cwid), per_group),
            pl.BlockSpec((None, 1, cwid), per_group),
            pl.BlockSpec((None, hpg, 1), per_group),
            pl.BlockSpec((None, hpg, 1), per_group),
            pl.BlockSpec((None, 1, gw), per_group),
            pl.BlockSpec((None, 1, gw), per_group),
            pl.BlockSpec(u3.shape, const2),
            pl.BlockSpec(ex.shape, const2),
            pl.BlockSpec(rep.shape, const2),
        ],
        out_specs=pl.BlockSpec((lc, gw), lambda b, g, c: (row(b, g, c), g)),
        out_shape=jax.ShapeDtypeStruct((t, d_inner), BF16),
        scratch_shapes=[
            pltpu.VMEM((D_STATE, gw), F32),
            pltpu.VMEM((ns, lc + 8, LANES), F32),
            pltpu.VMEM((ns, lc, LANES), F32),
            pltpu.VMEM((n_chunks, 6 * hpg, q_len), BF16),
            pltpu.VMEM((n_chunks, hpg, q_len), F32),
            pltpu.VMEM((n_chunks, gw), F32),
            pltpu.VMEM((lc, hpg * q_len), F32),
            pltpu.VMEM((lc, q_len + D_STATE), F32),
            pltpu.VMEM((n_chunks, D_STATE, gw), F32),
        ],
        compiler_params=_cparams(("parallel", "parallel", "arbitrary"), 32),
        name="ssd",
    )(proj, proj, proj, proj, dt_t, cw, cb, dtb, alog, dsk, ng, u3, ex, rep)


def _pool_kernel(u_ref, halo_ref, wg_ref, sc_ref, o_ref, buf, *, tp, tiles_per_seq, gdim):
    i = pl.program_id(0)
    seq_tile = i % tiles_per_seq
    keep = (seq_tile != 0).astype(F32)
    buf[0:POOL_HALO, :] = halo_ref[...].astype(F32) * keep
    buf[POOL_HALO:POOL_HALO + tp, :] = u_ref[...].astype(F32)

    rc = 128
    for r0 in range(0, tp, rc):
        pos = (seq_tile * tp + r0 + 1
               + lax.broadcasted_iota(jnp.int32, (rc, 1), 0)).astype(F32)
        for gi, win in enumerate(POOL_WINDOWS):
            cols = slice(gi * gdim, (gi + 1) * gdim)
            base = POOL_HALO + r0
            cur = buf[base:base + rc, cols]
            s = cur
            for k in range(1, win):
                s = s + buf[base - k:base - k + rc, cols]
            pooled = (s / jnp.minimum(pos, float(win)) - cur).astype(BF16)
            mixed = jnp.dot(pooled, wg_ref[gi], preferred_element_type=F32)
            o_ref[r0:r0 + rc, cols] = (mixed * sc_ref[:, cols]).astype(o_ref.dtype)


def _pool(proj, w_grp, scale, *, seq, d_pool, col0, tp=512):
    t = proj.shape[0]
    n_groups = len(POOL_WINDOWS)
    gdim = d_pool // n_groups
    tiles_per_seq = seq // tp
    cblk = col0 // d_pool
    hb = tp // POOL_HALO
    return pl.pallas_call(
        functools.partial(_pool_kernel, tp=tp, tiles_per_seq=tiles_per_seq, gdim=gdim),
        grid=(t // tp,),
        in_specs=[
            pl.BlockSpec((tp, d_pool), lambda i: (i, cblk)),
            pl.BlockSpec((POOL_HALO, d_pool), lambda i: (jnp.maximum(i * hb - 1, 0), cblk)),
            pl.BlockSpec((n_groups, gdim, gdim), lambda i: (0, 0, 0)),
            pl.BlockSpec((1, d_pool), lambda i: (0, 0)),
        ],
        out_specs=pl.BlockSpec((tp, d_pool), lambda i: (i, 0)),
        out_shape=jax.ShapeDtypeStruct((t, d_pool), BF16),
        scratch_shapes=[pltpu.VMEM((tp + POOL_HALO, d_pool), F32)],
        compiler_params=_cparams(("parallel",), 32),
        name="pool",
    )(proj, proj, w_grp, scale)


def _merge_kernel(ys_ref, yp_ref, gs_ref, gp_ref, ws_ref, wp_ref, o_ref):
    a = jnp.dot(ys_ref[...], ws_ref[...], preferred_element_type=F32)
    b = jnp.dot(yp_ref[...], wp_ref[...], preferred_element_type=F32)
    merged = (jax.nn.sigmoid(gs_ref[...].astype(F32)) * a
              + jax.nn.sigmoid(gp_ref[...].astype(F32)) * b)
    o_ref[...] = merged.astype(o_ref.dtype)


def _merge(y_ssm, y_pool, proj, w_so, w_po, *, gs_col0, gp_col0, tm=1024, tn=512):
    t, d_inner = y_ssm.shape
    d_pool = y_pool.shape[1]
    d = w_so.shape[1]
    gs_blk = gs_col0 // tn
    gp_blk = gp_col0 // tn
    return pl.pallas_call(
        _merge_kernel,
        grid=(t // tm, d // tn),
        in_specs=[
            pl.BlockSpec((tm, d_inner), lambda i, j: (i, 0)),
            pl.BlockSpec((tm, d_pool), lambda i, j: (i, 0)),
            pl.BlockSpec((tm, tn), lambda i, j: (i, gs_blk + j)),
            pl.BlockSpec((tm, tn), lambda i, j: (i, gp_blk + j)),
            pl.BlockSpec((d_inner, tn), lambda i, j: (0, j)),
            pl.BlockSpec((d_pool, tn), lambda i, j: (0, j)),
        ],
        out_specs=pl.BlockSpec((tm, tn), lambda i, j: (i, j)),
        out_shape=jax.ShapeDtypeStruct((t, d), BF16),
        compiler_params=_cparams(("parallel", "arbitrary"), 48),
        name="merge",
    )(y_ssm, y_pool, proj, proj, w_so, w_po)


def _mix_kernel(m_ref, x_ref, w_ref, gpost_ref, gpre_ref, x1_ref, h2_ref):
    mix = jnp.dot(m_ref[...], w_ref[...], preferred_element_type=F32)
    x1 = x_ref[...] + _rms(mix, gpost_ref[...])
    x1_ref[...] = x1
    h2_ref[...] = _rms(x1, gpre_ref[...]).astype(h2_ref.dtype)


def _mix(merged, x2d, w_mix, g_post, g_pre, *, tm=512):
    t, d = x2d.shape
    return pl.pallas_call(
        _mix_kernel,
        grid=(t // tm,),
        in_specs=[
            pl.BlockSpec((tm, d), lambda i: (i, 0)),
            pl.BlockSpec((tm, d), lambda i: (i, 0)),
            pl.BlockSpec((d, d), lambda i: (0, 0)),
            pl.BlockSpec((1, d), lambda i: (0, 0)),
            pl.BlockSpec((1, d), lambda i: (0, 0)),
        ],
        out_specs=[
            pl.BlockSpec((tm, d), lambda i: (i, 0)),
            pl.BlockSpec((tm, d), lambda i: (i, 0)),
        ],
        out_shape=[
            jax.ShapeDtypeStruct((t, d), F32),
            jax.ShapeDtypeStruct((t, d), BF16),
        ],
        compiler_params=_cparams(("parallel",), 48),
        name="mix",
    )(merged, x2d, w_mix, g_post, g_pre)


def _ffn_kernel(h_ref, x1_ref, wg_ref, wu_ref, wo_ref, g_ref, o_ref):
    j = pl.program_id(1)
    h = h_ref[...]
    gate = jnp.dot(h, wg_ref[...], preferred_element_type=F32)
    up = jnp.dot(h, wu_ref[...], preferred_element_type=F32)
    act = (_silu(gate) * up).astype(BF16)
    part = jnp.dot(act, wo_ref[...], preferred_element_type=F32)

    @pl.when(j == 0)
    def _():
        o_ref[...] = part

    @pl.when(j != 0)
    def _():
        o_ref[...] += part

    @pl.when(j == pl.num_programs(1) - 1)
    def _():
        o_ref[...] = x1_ref[...] + _rms(o_ref[...], g_ref[...])


def _ffn(h2, x1, w_ffn_in, w_out, g_post, *, tm=512, tf=512):
    t, d = x1.shape
    d_ff = w_out.shape[0]
    n_f = d_ff // tf
    return pl.pallas_call(
        _ffn_kernel,
        grid=(t // tm, n_f),
        in_specs=[
            pl.BlockSpec((tm, d), lambda i, j: (i, 0)),
            pl.BlockSpec((tm, d), lambda i, j: (i, 0)),
            pl.BlockSpec((d, tf), lambda i, j: (0, j)),
            pl.BlockSpec((d, tf), lambda i, j: (0, n_f + j)),
            pl.BlockSpec((tf, d), lambda i, j: (j, 0)),
            pl.BlockSpec((1, d), lambda i, j: (0, 0)),
        ],
        out_specs=pl.BlockSpec((tm, d), lambda i, j: (i, 0)),
        out_shape=jax.ShapeDtypeStruct((t, d), F32),
        compiler_params=_cparams(("parallel", "arbitrary"), 48),
        name="ffn",
    )(h2, x1, w_ffn_in, w_ffn_in, w_out, g_post)


def _ple_kernel(x_ref, p_ref, wg_ref, wp_ref, o_ref):
    x = x_ref[...]
    gate = jnp.dot(x.astype(BF16), wg_ref[...], preferred_element_type=F32)
    ple = jnp.dot(p_ref[...].astype(BF16), wp_ref[...], preferred_element_type=F32)
    o_ref[...] = x + jax.nn.sigmoid(gate) * ple


def _ple(x2, p2d, w_gate, w_proj, *, tm=512):
    t, d = x2.shape
    k = p2d.shape[1]
    return pl.pallas_call(
        _ple_kernel,
        grid=(t // tm,),
        in_specs=[
            pl.BlockSpec((tm, d), lambda i: (i, 0)),
            pl.BlockSpec((tm, k), lambda i: (i, 0)),
            pl.BlockSpec((d, d), lambda i: (0, 0)),
            pl.BlockSpec((k, d), lambda i: (0, 0)),
        ],
        out_specs=pl.BlockSpec((tm, d), lambda i: (i, 0)),
        out_shape=jax.ShapeDtypeStruct((t, d), F32),
        compiler_params=_cparams(("parallel",), 48),
        name="ple",
    )(x2, p2d, w_gate, w_proj)


def kernel(x, p, w_in, conv_w, conv_b, dt_bias, a_log, d_skip, ssm_norm_g, w_ssm_out,
           w_pool_grp, pool_scale, w_pool_out, w_mix_out, g_pre_mix, g_post_mix,
           g_pre_ffn, g_post_ffn, w_ffn_in, w_ffn_out, w_ple_proj, w_ple_gate):
    batch, seq, d = x.shape
    depth = w_in.shape[0]
    d_inner = w_ssm_out.shape[1]
    d_pool = w_pool_out.shape[1]
    heads = dt_bias.shape[1]
    groups = SSM_GROUPS
    hpg = heads // groups
    gw = d_inner // groups
    n_bc = groups * D_STATE
    conv_dim = d_inner + 2 * n_bc
    t = batch * seq
    assert heads * SSM_HEAD_DIM == d_inner and conv_w.shape[2] == conv_dim
    assert w_in.shape[2] == d_inner + conv_dim + heads + d_pool + 2 * d

    dt_col0 = d_inner + conv_dim
    pool_col0 = dt_col0
    gs_col0 = pool_col0 + d_pool
    gp_col0 = gs_col0 + d

    x2d = x.reshape(t, d)
    for li in range(depth):
        wi = w_in[li]
        w_main = jnp.concatenate([wi[:, :dt_col0], wi[:, dt_col0 + heads:]], axis=1).astype(BF16)
        w_dt_t = wi[:, dt_col0:dt_col0 + heads].T.astype(BF16)

        def per_group_cols(v):
            xs = v[..., :d_inner].reshape(v.shape[:-1] + (groups, gw))
            bm = v[..., d_inner:d_inner + n_bc].reshape(v.shape[:-1] + (groups, D_STATE))
            cm = v[..., d_inner + n_bc:].reshape(v.shape[:-1] + (groups, D_STATE))
            cat = jnp.concatenate([xs, bm, cm], axis=-1)
            return jnp.moveaxis(cat, -2, 0)
        cw = per_group_cols(conv_w[li])
        cb = per_group_cols(conv_b[li][None, :])
        dtb = dt_bias[li].reshape(groups, hpg, 1)
        alog = a_log[li].reshape(groups, hpg, 1)
        dsk = jnp.repeat(d_skip[li], SSM_HEAD_DIM).reshape(groups, 1, gw)
        ng = ssm_norm_g[li].reshape(groups, 1, gw)

        proj, dt_t = _in_proj(x2d, g_pre_mix[li][None, :], w_main, w_dt_t)
        y_ssm = _ssd(proj, dt_t, cw, cb, dtb, alog, dsk, ng,
                     batch=batch, seq=seq, d_inner=d_inner)
        y_pool = _pool(proj, w_pool_grp[li].astype(BF16), pool_scale[li][None, :],
                       seq=seq, d_pool=d_pool, col0=pool_col0)
        merged = _merge(y_ssm, y_pool, proj, w_ssm_out[li].astype(BF16),
                        w_pool_out[li].astype(BF16), gs_col0=gs_col0, gp_col0=gp_col0)
        x1, h2 = _mix(merged, x2d, w_mix_out[li].astype(BF16),
                      g_post_mix[li][None, :], g_pre_ffn[li][None, :])
        x2 = _ffn(h2, x1, w_ffn_in[li].astype(BF16), w_ffn_out[li].astype(BF16),
                  g_post_ffn[li][None, :])
        x2d = _ple(x2, p[li].reshape(t, -1), w_ple_gate[li].astype(BF16),
                   w_ple_proj[li].astype(BF16))
    return x2d.reshape(batch, seq, d)
```

```python
import functools

import jax
import jax.numpy as jnp
import numpy as np
from jax import lax
from jax.experimental import pallas as pl
from jax.experimental.pallas import tpu as pltpu

F32 = jnp.float32
BF16 = jnp.bfloat16

EPS = 1e-6
SSM_HEAD_DIM = 64
SSM_GROUPS = 8
D_STATE = 128
CONV_WIDTH = 4
POOL_WINDOWS = (2, 4, 8, 16)
POOL_HALO = 16
SSD_Q = 128
LANES = 128
CONV_PHASES = 4
ROW_SUB = 256

LOG2_E = 1.4426950408889634

MIB = 1024 * 1024


def _cparams(semantics, vmem_mib):
    return pltpu.CompilerParams(dimension_semantics=semantics,
                                vmem_limit_bytes=vmem_mib * MIB)


def _rms(v, g):
    ms = jnp.mean(v * v, axis=-1, keepdims=True)
    return v * lax.rsqrt(ms + EPS) * g


def _silu(v):
    return v * jax.nn.sigmoid(v)


def _silu_of_twice(h):
    return h + h * jnp.tanh(h)


def _softplus(v):
    return jnp.maximum(v, 0.0) + jnp.log1p(jnp.exp(-jnp.abs(v)))


def _split_terms(v, n_terms):
    pieces = []
    rem = v
    for _ in range(n_terms - 1):
        hi = rem.astype(BF16).astype(F32)
        pieces.append(hi)
        rem = rem - hi
    pieces.append(rem)
    return pieces


def _in_proj_kernel(x_ref, g_ref, w_ref, wdt_ref, o_ref, dt_ref, h_scr, *, n_chunks):
    @pl.when(pl.program_id(1) == 0)
    def _():
        def body(q, carry):
            rows = pl.ds(pl.multiple_of(q * SSD_Q, SSD_Q), SSD_Q)
            h = _rms(x_ref[rows, :], g_ref[...]).astype(BF16)
            h_scr[rows, :] = h
            dt_ref[q] = lax.dot_general(wdt_ref[...], h, (((1,), (1,)), ((), ())),
                                        preferred_element_type=F32)
            return carry
        lax.fori_loop(0, n_chunks, body, 0)

    o_ref[...] = jnp.dot(h_scr[...], w_ref[...],
                         preferred_element_type=F32).astype(o_ref.dtype)


def _in_proj(x2d, g, w_main, w_dt_t, *, tm=1024, tn=1024):
    t, d = x2d.shape
    n = w_main.shape[1]
    heads = w_dt_t.shape[0]
    n_chunks = tm // SSD_Q
    return pl.pallas_call(
        functools.partial(_in_proj_kernel, n_chunks=n_chunks),
        grid=(t // tm, n // tn),
        in_specs=[
            pl.BlockSpec((tm, d), lambda i, j: (i, 0)),
            pl.BlockSpec((1, d), lambda i, j: (0, 0)),
            pl.BlockSpec((d, tn), lambda i, j: (0, j)),
            pl.BlockSpec((heads, d), lambda i, j: (0, 0)),
        ],
        out_specs=[
            pl.BlockSpec((tm, tn), lambda i, j: (i, j)),
            pl.BlockSpec((n_chunks, heads, SSD_Q), lambda i, j: (i, 0, 0)),
        ],
        out_shape=[
            jax.ShapeDtypeStruct((t, n), BF16),
            jax.ShapeDtypeStruct((t // SSD_Q, heads, SSD_Q), F32),
        ],
        scratch_shapes=[pltpu.VMEM((tm, d), BF16)],
        compiler_params=_cparams(("parallel", "arbitrary"), 48),
        name="in_proj",
    )(x2d, g, w_main, w_dt_t)


def _ssd_kernel(z_ref, xs_ref, b_ref, c_ref, dt_ref, cw_ref, cb_ref, dtb_ref, alog_ref,
                dsk_ref, ng_ref, u3_ref, ex_ref, rep_ref, y_ref,
                state_scr, ubuf, xc, lhs_scr, cumj_scr, dl_scr, ci_scr, g_scr, s_scr,
                *, lc, gw):
    q_len = SSD_Q
    n_chunks = lc // q_len
    hpg = gw // SSM_HEAD_DIM
    ns_x = gw // LANES
    ns = ns_x + 2
    halo = 8
    pr = q_len // CONV_PHASES

    @pl.when(pl.program_id(2) == 0)
    def _():
        state_scr[...] = jnp.zeros_like(state_scr)
        ubuf[:, 0:halo, :] = jnp.zeros((ns, halo, LANES), F32)

    for s in range(ns_x):
        ubuf[s, halo:halo + lc, :] = xs_ref[:, s * LANES:(s + 1) * LANES].astype(F32)
    ubuf[ns_x, halo:halo + lc, :] = b_ref[...].astype(F32)
    ubuf[ns_x + 1, halo:halo + lc, :] = c_ref[...].astype(F32)

    dt_all = _softplus(dt_ref[...] + dtb_ref[...][None]).reshape(n_chunks * hpg, q_len)
    a_all = jnp.broadcast_to(-jnp.exp(alog_ref[...])[None],
                             (n_chunks, hpg, 1)).reshape(n_chunks * hpg, 1)
    da3 = jnp.concatenate(_split_terms(dt_all * a_all, 3), axis=1).astype(BF16)
    cum2 = jnp.dot(da3, u3_ref[...], preferred_element_type=F32) * LOG2_E
    cum2_last = cum2[:, q_len - 1:q_len]
    te_all = jnp.exp2(cum2_last - cum2) * dt_all
    cumj_scr[...] = (cum2 - jnp.log2(dt_all)).reshape(n_chunks, hpg, q_len)
    pieces = [p.reshape(n_chunks, hpg, q_len)
              for p in _split_terms(cum2, 3) + _split_terms(te_all, 2)]
    pieces.append(jnp.zeros((n_chunks, hpg, q_len), F32))
    lhs_scr[...] = jnp.concatenate(pieces, axis=1).astype(BF16)
    dl_scr[...] = jnp.sum(jnp.exp2(cum2_last).reshape(n_chunks, hpg, 1) * rep_ref[...][None],
                          axis=1)

    def conv(k):
        r0 = pl.multiple_of(k * q_len, q_len)
        for s in range(ns):
            lanes = slice(s * LANES, (s + 1) * LANES)
            taps = {d: ubuf.at[s][pl.ds(r0 + halo + d, pr, stride=CONV_PHASES), :]
                    for d in range(1 - CONV_WIDTH, CONV_PHASES)}
            for ph in range(CONV_PHASES):
                acc = cb_ref[:, lanes]
                for kk in range(CONV_WIDTH):
                    acc = acc + cw_ref[kk:kk + 1, lanes] * taps[ph - (CONV_WIDTH - 1) + kk]
                xc.at[s][pl.ds(r0 + ph, pr, stride=CONV_PHASES), :] = _silu_of_twice(acc)

    def conv_pair(kk, carry):
        conv(2 * kk)
        conv(2 * kk + 1)
        return carry

    lax.fori_loop(0, n_chunks // 2, conv_pair, 0)
    ubuf[:, 0:halo, :] = ubuf[:, lc:lc + halo, :]

    dsk = dsk_ref[...]
    ng = ng_ref[...]
    row_i = lax.broadcasted_iota(jnp.int32, (q_len, q_len), 0)
    col_j = lax.broadcasted_iota(jnp.int32, (q_len, q_len), 1)
    causal = row_i >= col_j
    low_half = lax.broadcasted_iota(jnp.int32, (1, LANES), 1) < SSM_HEAD_DIM

    def chunk_rows(k):
        start = k * q_len
        return pl.ds(start if isinstance(k, int) else pl.multiple_of(start, q_len), q_len)

    def load_x(rows):
        return jnp.concatenate([xc[s, rows, :] for s in range(ns_x)], axis=1)

    def prep(k, slot):
        rows = chunk_rows(k)
        xq = load_x(rows)
        cq = xc[ns_x + 1, rows, :]
        bq16 = xc[ns_x, rows, :].astype(BF16)
        cq16 = cq.astype(BF16)

        xp = lax.dot_general(lhs_scr[k], ex_ref[...], (((0,), (0,)), ((), ())),
                             preferred_element_type=F32)
        ci_scr[slot] = xp[:, 0:hpg * q_len]
        te_exp = xp[:, hpg * q_len:hpg * q_len + gw]

        cb = lax.dot_general(cq16, bq16, (((1,), (1,)), ((), ())),
                             preferred_element_type=F32)
        g_scr[slot] = jnp.concatenate([cb, cq], axis=1)

        xte = (xq * te_exp).astype(BF16)
        s_scr[slot] = lax.dot_general(bq16, xte, (((0,), (0,)), ((), ())),
                                      preferred_element_type=F32)

    def heads(k, slot):
        rows = chunk_rows(k)
        xq = load_x(rows)
        g_mat = g_scr[slot]
        cum_j = cumj_scr[k]
        st = state_scr[...]
        rhs = jnp.concatenate([xq.astype(BF16), st.astype(BF16)], axis=0)
        ys = []
        for m in range(hpg // 2):
            rp = rhs[:, m * LANES:(m + 1) * LANES]
            parts = []
            for r in (2 * m, 2 * m + 1):
                ci = ci_scr[slot, :, r * q_len:(r + 1) * q_len]
                seg = jnp.where(causal, ci - cum_j[r:r + 1, :], -jnp.inf)
                lhs = (g_mat * jnp.exp2(jnp.concatenate([seg, ci], axis=1))).astype(BF16)
                parts.append(jnp.dot(lhs, rp, preferred_element_type=F32))
            ys.append(jnp.where(low_half, parts[0], parts[1]))
        y_mm = jnp.concatenate(ys, axis=1)

        state_scr[...] = st * dl_scr[pl.ds(k, 1), :] + s_scr[slot]

        z_half = z_ref[rows, :].astype(F32)
        y = (y_mm + dsk * xq) * _silu_of_twice(z_half)
        y_ref[rows, :] = _rms(y, ng).astype(y_ref.dtype)

    prep(0, 0)

    def pair(kk, carry):
        k0 = 2 * kk
        heads(k0, 0)
        prep(k0 + 1, 1)
        heads(k0 + 1, 1)
        prep(jnp.minimum(k0 + 2, n_chunks - 1), 0)
        return carry

    lax.fori_loop(0, n_chunks // 2, pair, 0)


def _ssd(proj, dt_t, cw, cb, dtb, alog, dsk, ng, *, batch, seq, d_inner, lc=1024):
    t = proj.shape[0]
    groups = SSM_GROUPS
    gw = d_inner // groups
    hpg = gw // SSM_HEAD_DIM
    q_len = SSD_Q
    n_tiles = seq // lc
    n_chunks = lc // q_len
    cwid = gw + 2 * D_STATE
    ns = cwid // LANES
    assert D_STATE == q_len == LANES and hpg % 2 == 0 and gw % LANES == 0
    xs_blk0 = d_inner // gw
    b_blk0 = 2 * d_inner // D_STATE
    c_blk0 = b_blk0 + groups

    tri = np.triu(np.ones((q_len, q_len), np.float32))
    u3 = jnp.asarray(np.concatenate([tri, tri, tri], axis=0), BF16)
    rep = np.kron(np.eye(hpg, dtype=np.float32), np.ones((1, SSM_HEAD_DIM), np.float32))
    rq = np.kron(np.eye(hpg, dtype=np.float32), np.ones((1, q_len), np.float32))
    zq = np.zeros_like(rq)
    zr = np.zeros_like(rep)
    ex = np.concatenate([
        np.concatenate([rq, zr], axis=1), np.concatenate([rq, zr], axis=1),
        np.concatenate([rq, zr], axis=1),
        np.concatenate([zq, rep], axis=1), np.concatenate([zq, rep], axis=1),
        np.concatenate([zq, zr], axis=1),
    ], axis=0)
    ex = jnp.asarray(ex, BF16)
    rep = jnp.asarray(rep, F32)

    row = lambda b, g, c: b * n_tiles + c
    const2 = lambda b, g, c: (0, 0)
    per_group = lambda b, g, c: (g, 0, 0)
    return pl.pallas_call(
        functools.partial(_ssd_kernel, lc=lc, gw=gw),
        grid=(batch, groups, n_tiles),
        in_specs=[
            pl.BlockSpec((lc, gw), lambda b, g, c: (row(b, g, c), g)),
            pl.BlockSpec((lc, gw), lambda b, g, c: (row(b, g, c), xs_blk0 + g)),
            pl.BlockSpec((lc, D_STATE), lambda b, g, c: (row(b, g, c), b_blk0 + g)),
            pl.BlockSpec((lc, D_STATE), lambda b, g, c: (row(b, g, c), c_blk0 + g)),
            pl.BlockSpec((lc // q_len, hpg, q_len), lambda b, g, c: (row(b, g, c), g, 0)),
            pl.BlockSpec((None, CONV_WIDTH, cwid), per_group),
            pl.BlockSpec((None, 1, cwid), per_group),
            pl.BlockSpec((None, hpg, 1), per_group),
            pl.BlockSpec((None, hpg, 1), per_group),
            pl.BlockSpec((None, 1, gw), per_group),
            pl.BlockSpec((None, 1, gw), per_group),
            pl.BlockSpec(u3.shape, const2),
            pl.BlockSpec(ex.shape, const2),
            pl.BlockSpec(rep.shape, const2),
        ],
        out_specs=pl.BlockSpec((lc, gw), lambda b, g, c: (row(b, g, c), g)),
        out_shape=jax.ShapeDtypeStruct((t, d_inner), BF16),
        scratch_shapes=[
            pltpu.VMEM((D_STATE, gw), F32),
            pltpu.VMEM((ns, lc + 8, LANES), F32),
            pltpu.VMEM((ns, lc, LANES), F32),
            pltpu.VMEM((n_chunks, 6 * hpg, q_len), BF16),
            pltpu.VMEM((n_chunks, hpg, q_len), F32),
            pltpu.VMEM((n_chunks, gw), F32),
            pltpu.VMEM((2, q_len, hpg * q_len), F32),
            pltpu.VMEM((2, q_len, q_len + D_STATE), F32),
            pltpu.VMEM((2, D_STATE, gw), F32),
        ],
        compiler_params=_cparams(("parallel", "parallel", "arbitrary"), 32),
        name="ssd",
    )(proj, proj, proj, proj, dt_t, cw, cb, dtb, alog, dsk, ng, u3, ex, rep)


def _pool_kernel(u_ref, halo_ref, wg_ref, sc_ref, o_ref, buf, *, tp, tiles_per_seq, gdim):
    i = pl.program_id(0)
    seq_tile = i % tiles_per_seq
    keep = (seq_tile != 0).astype(F32)
    buf[0:POOL_HALO, :] = halo_ref[...].astype(F32) * keep
    buf[POOL_HALO:POOL_HALO + tp, :] = u_ref[...].astype(F32)

    rc = 128
    for r0 in range(0, tp, rc):
        pos = (seq_tile * tp + r0 + 1
               + lax.broadcasted_iota(jnp.int32, (rc, 1), 0)).astype(F32)
        for gi, win in enumerate(POOL_WINDOWS):
            cols = slice(gi * gdim, (gi + 1) * gdim)
            base = POOL_HALO + r0
            cur = buf[base:base + rc, cols]
            s = cur
            for k in range(1, win):
                s = s + buf[base - k:base - k + rc, cols]
            pooled = (s / jnp.minimum(pos, float(win)) - cur).astype(BF16)
            mixed = jnp.dot(pooled, wg_ref[gi], preferred_element_type=F32)
            o_ref[r0:r0 + rc, cols] = (mixed * sc_ref[:, cols]).astype(o_ref.dtype)


def _pool(proj, w_grp, scale, *, seq, d_pool, col0, tp=512):
    t = proj.shape[0]
    n_groups = len(POOL_WINDOWS)
    gdim = d_pool // n_groups
    tiles_per_seq = seq // tp
    cblk = col0 // d_pool
    hb = tp // POOL_HALO
    return pl.pallas_call(
        functools.partial(_pool_kernel, tp=tp, tiles_per_seq=tiles_per_seq, gdim=gdim),
        grid=(t // tp,),
        in_specs=[
            pl.BlockSpec((tp, d_pool), lambda i: (i, cblk)),
            pl.BlockSpec((POOL_HALO, d_pool), lambda i: (jnp.maximum(i * hb - 1, 0), cblk)),
            pl.BlockSpec((n_groups, gdim, gdim), lambda i: (0, 0, 0)),
            pl.BlockSpec((1, d_pool), lambda i: (0, 0)),
        ],
        out_specs=pl.BlockSpec((tp, d_pool), lambda i: (i, 0)),
        out_shape=jax.ShapeDtypeStruct((t, d_pool), BF16),
        scratch_shapes=[pltpu.VMEM((tp + POOL_HALO, d_pool), F32)],
        compiler_params=_cparams(("parallel",), 32),
        name="pool",
    )(proj, proj, w_grp, scale)


def _merge_kernel(ys_ref, yp_ref, gs_ref, gp_ref, ws_ref, wp_ref, o_ref):
    a = jnp.dot(ys_ref[...], ws_ref[...], preferred_element_type=F32)
    b = jnp.dot(yp_ref[...], wp_ref[...], preferred_element_type=F32)
    merged = (jax.nn.sigmoid(gs_ref[...].astype(F32)) * a
              + jax.nn.sigmoid(gp_ref[...].astype(F32)) * b)
    o_ref[...] = merged.astype(o_ref.dtype)


def _merge(y_ssm, y_pool, proj, w_so, w_po, *, gs_col0, gp_col0, tm=1024, tn=512):
    t, d_inner = y_ssm.shape
    d_pool = y_pool.shape[1]
    d = w_so.shape[1]
    gs_blk = gs_col0 // tn
    gp_blk = gp_col0 // tn
    return pl.pallas_call(
        _merge_kernel,
        grid=(t // tm, d // tn),
        in_specs=[
            pl.BlockSpec((tm, d_inner), lambda i, j: (i, 0)),
            pl.BlockSpec((tm, d_pool), lambda i, j: (i, 0)),
            pl.BlockSpec((tm, tn), lambda i, j: (i, gs_blk + j)),
            pl.BlockSpec((tm, tn), lambda i, j: (i, gp_blk + j)),
            pl.BlockSpec((d_inner, tn), lambda i, j: (0, j)),
            pl.BlockSpec((d_pool, tn), lambda i, j: (0, j)),
        ],
        out_specs=pl.BlockSpec((tm, tn), lambda i, j: (i, j)),
        out_shape=jax.ShapeDtypeStruct((t, d), BF16),
        compiler_params=_cparams(("parallel", "arbitrary"), 48),
        name="merge",
    )(y_ssm, y_pool, proj, proj, w_so, w_po)


def _mix_kernel(m_ref, x_ref, w_ref, gpost_ref, gpre_ref, x1_ref, h2_ref):
    for r0 in range(0, m_ref.shape[0], ROW_SUB):
        rows = slice(r0, r0 + ROW_SUB)
        mix = jnp.dot(m_ref[rows, :], w_ref[...], preferred_element_type=F32)
        x1 = x_ref[rows, :] + _rms(mix, gpost_ref[...])
        x1_ref[rows, :] = x1
        h2_ref[rows, :] = _rms(x1, gpre_ref[...]).astype(h2_ref.dtype)


def _mix(merged, x2d, w_mix, g_post, g_pre, *, tm=512):
    t, d = x2d.shape
    return pl.pallas_call(
        _mix_kernel,
        grid=(t // tm,),
        in_specs=[
            pl.BlockSpec((tm, d), lambda i: (i, 0)),
            pl.BlockSpec((tm, d), lambda i: (i, 0)),
            pl.BlockSpec((d, d), lambda i: (0, 0)),
            pl.BlockSpec((1, d), lambda i: (0, 0)),
            pl.BlockSpec((1, d), lambda i: (0, 0)),
        ],
        out_specs=[
            pl.BlockSpec((tm, d), lambda i: (i, 0)),
            pl.BlockSpec((tm, d), lambda i: (i, 0)),
        ],
        out_shape=[
            jax.ShapeDtypeStruct((t, d), F32),
            jax.ShapeDtypeStruct((t, d), BF16),
        ],
        compiler_params=_cparams(("parallel",), 48),
        name="mix",
    )(merged, x2d, w_mix, g_post, g_pre)


def _ffn_up_kernel(h_ref, wg_ref, wu_ref, o_ref):
    h = h_ref[...]
    gate = jnp.dot(h, wg_ref[...], preferred_element_type=F32)
    up = jnp.dot(h, wu_ref[...], preferred_element_type=F32)
    o_ref[...] = (_silu(gate) * up).astype(o_ref.dtype)


def _ffn_up(h2, w_ffn_in, *, tm=1024, tf=512):
    t, d = h2.shape
    d_ff = w_ffn_in.shape[1] // 2
    n_f = d_ff // tf
    return pl.pallas_call(
        _ffn_up_kernel,
        grid=(t // tm, n_f),
        in_specs=[
            pl.BlockSpec((tm, d), lambda i, j: (i, 0)),
            pl.BlockSpec((d, tf), lambda i, j: (0, j)),
            pl.BlockSpec((d, tf), lambda i, j: (0, n_f + j)),
        ],
        out_specs=pl.BlockSpec((tm, tf), lambda i, j: (i, j)),
        out_shape=jax.ShapeDtypeStruct((t, d_ff), BF16),
        compiler_params=_cparams(("parallel", "arbitrary"), 40),
        name="ffn_up",
    )(h2, w_ffn_in, w_ffn_in)


def _ffn_down_kernel(a_ref, x1_ref, wo_ref, g_ref, o_ref, f_scr, *, n_col, tn):
    j = pl.program_id(1)
    f_scr[j] = jnp.dot(a_ref[...], wo_ref[...], preferred_element_type=F32)

    @pl.when(j == n_col - 1)
    def _():
        d = n_col * tn
        ss = None
        for c in range(n_col):
            fc = f_scr[c]
            part = jnp.sum(fc * fc, axis=-1, keepdims=True)
            ss = part if ss is None else ss + part
        inv = lax.rsqrt(ss / d + EPS)
        for c in range(n_col):
            cols = slice(c * tn, (c + 1) * tn)
            o_ref[:, cols] = x1_ref[:, cols] + f_scr[c] * inv * g_ref[:, cols]


def _ffn_down(act, x1, w_out, g_post, *, tm=512, tn=512):
    t, d = x1.shape
    d_ff = w_out.shape[0]
    n_col = d // tn
    return pl.pallas_call(
        functools.partial(_ffn_down_kernel, n_col=n_col, tn=tn),
        grid=(t // tm, n_col),
        in_specs=[
            pl.BlockSpec((tm, d_ff), lambda i, j: (i, 0)),
            pl.BlockSpec((tm, d), lambda i, j: (i, 0)),
            pl.BlockSpec((d_ff, tn), lambda i, j: (0, j)),
            pl.BlockSpec((1, d), lambda i, j: (0, 0)),
        ],
        out_specs=pl.BlockSpec((tm, d), lambda i, j: (i, 0)),
        out_shape=jax.ShapeDtypeStruct((t, d), F32),
        scratch_shapes=[pltpu.VMEM((n_col, tm, tn), F32)],
        compiler_params=_cparams(("parallel", "arbitrary"), 52),
        name="ffn_down",
    )(act, x1, w_out, g_post)


def _ple_kernel(x_ref, p_ref, wg_ref, wp_ref, o_ref):
    for r0 in range(0, x_ref.shape[0], ROW_SUB):
        rows = slice(r0, r0 + ROW_SUB)
        x = x_ref[rows, :]
        gate = jnp.dot(x.astype(BF16), wg_ref[...], preferred_element_type=F32)
        ple = jnp.dot(p_ref[rows, :].astype(BF16), wp_ref[...], preferred_element_type=F32)
        o_ref[rows, :] = x + jax.nn.sigmoid(gate) * ple


def _ple(x2, p2d, w_gate, w_proj, *, tm=512):
    t, d = x2.shape
    k = p2d.shape[1]
    return pl.pallas_call(
        _ple_kernel,
        grid=(t // tm,),
        in_specs=[
            pl.BlockSpec((tm, d), lambda i: (i, 0)),
            pl.BlockSpec((tm, k), lambda i: (i, 0)),
            pl.BlockSpec((d, d), lambda i: (0, 0)),
            pl.BlockSpec((k, d), lambda i: (0, 0)),
        ],
        out_specs=pl.BlockSpec((tm, d), lambda i: (i, 0)),
        out_shape=jax.ShapeDtypeStruct((t, d), F32),
        compiler_params=_cparams(("parallel",), 48),
        name="ple",
    )(x2, p2d, w_gate, w_proj)


def kernel(x, p, w_in, conv_w, conv_b, dt_bias, a_log, d_skip, ssm_norm_g, w_ssm_out,
           w_pool_grp, pool_scale, w_pool_out, w_mix_out, g_pre_mix, g_post_mix,
           g_pre_ffn, g_post_ffn, w_ffn_in, w_ffn_out, w_ple_proj, w_ple_gate):
    batch, seq, d = x.shape
    depth = w_in.shape[0]
    d_inner = w_ssm_out.shape[1]
    d_pool = w_pool_out.shape[1]
    heads = dt_bias.shape[1]
    groups = SSM_GROUPS
    hpg = heads // groups
    gw = d_inner // groups
    n_bc = groups * D_STATE
    conv_dim = d_inner + 2 * n_bc
    t = batch * seq
    assert heads * SSM_HEAD_DIM == d_inner and conv_w.shape[2] == conv_dim
    assert w_in.shape[2] == d_inner + conv_dim + heads + d_pool + 2 * d

    dt_col0 = d_inner + conv_dim
    pool_col0 = dt_col0
    gs_col0 = pool_col0 + d_pool
    gp_col0 = gs_col0 + d

    x2d = x.reshape(t, d)
    for li in range(depth):
        wi = w_in[li]
        w_main = jnp.concatenate([0.5 * wi[:, :d_inner], wi[:, d_inner:dt_col0],
                                  wi[:, dt_col0 + heads:]], axis=1).astype(BF16)
        w_dt_t = wi[:, dt_col0:dt_col0 + heads].T.astype(BF16)

        def per_group_cols(v):
            xs = v[..., :d_inner].reshape(v.shape[:-1] + (groups, gw))
            bm = v[..., d_inner:d_inner + n_bc].reshape(v.shape[:-1] + (groups, D_STATE))
            cm = v[..., d_inner + n_bc:].reshape(v.shape[:-1] + (groups, D_STATE))
            cat = jnp.concatenate([xs, bm, cm], axis=-1)
            return jnp.moveaxis(cat, -2, 0)
        cw = per_group_cols(0.5 * conv_w[li])
        cb = per_group_cols(0.5 * conv_b[li][None, :])
        dtb = dt_bias[li].reshape(groups, hpg, 1)
        alog = a_log[li].reshape(groups, hpg, 1)
        dsk = jnp.repeat(d_skip[li], SSM_HEAD_DIM).reshape(groups, 1, gw)
        ng = ssm_norm_g[li].reshape(groups, 1, gw)

        proj, dt_t = _in_proj(x2d, g_pre_mix[li][None, :], w_main, w_dt_t)
        y_ssm = _ssd(proj, dt_t, cw, cb, dtb, alog, dsk, ng,
                     batch=batch, seq=seq, d_inner=d_inner)
        y_pool = _pool(proj, w_pool_grp[li].astype(BF16), pool_scale[li][None, :],
                       seq=seq, d_pool=d_pool, col0=pool_col0)
        merged = _merge(y_ssm, y_pool, proj, w_ssm_out[li].astype(BF16),
                        w_pool_out[li].astype(BF16), gs_col0=gs_col0, gp_col0=gp_col0)
        x1, h2 = _mix(merged, x2d, w_mix_out[li].astype(BF16),
                      g_post_mix[li][None, :], g_pre_ffn[li][None, :])
        act = _ffn_up(h2, w_ffn_in[li].astype(BF16))
        x2 = _ffn_down(act, x1, w_ffn_out[li].astype(BF16), g_post_ffn[li][None, :])
        x2d = _ple(x2, p[li].reshape(t, -1), w_ple_gate[li].astype(BF16),
                   w_ple_proj[li].astype(BF16))
    return x2d.reshape(batch, seq, d)
```

```python
import functools

import jax
import jax.numpy as jnp
import numpy as np
from jax import lax
from jax.experimental import pallas as pl
from jax.experimental.pallas import tpu as pltpu

F32 = jnp.float32
BF16 = jnp.bfloat16

EPS = 1e-6
SSM_HEAD_DIM = 64
SSM_GROUPS = 8
D_STATE = 128
CONV_WIDTH = 4
POOL_WINDOWS = (2, 4, 8, 16)
POOL_HALO = 16
SSD_Q = 128
LANES = 128
CONV_PHASES = 4
ROW_SUB = 256

LOG2_E = 1.4426950408889634

MIB = 1024 * 1024


def _cparams(semantics, vmem_mib):
    return pltpu.CompilerParams(dimension_semantics=semantics,
                                vmem_limit_bytes=vmem_mib * MIB)


def _rms(v, g):
    ms = jnp.mean(v * v, axis=-1, keepdims=True)
    return v * lax.rsqrt(ms + EPS) * g


def _silu(v):
    return v * jax.nn.sigmoid(v)


def _silu_of_twice(h):
    return h + h * jnp.tanh(h)


def _softplus(v):
    return jnp.maximum(v, 0.0) + jnp.log1p(jnp.exp(-jnp.abs(v)))


def _split_terms(v, n_terms):
    pieces = []
    rem = v
    for _ in range(n_terms - 1):
        hi = rem.astype(BF16).astype(F32)
        pieces.append(hi)
        rem = rem - hi
    pieces.append(rem)
    return pieces


def _in_proj_kernel(x_ref, g_ref, wa_ref, wb_ref, wdt_ref, o_ref, dt_ref, h_scr,
                    *, n_chunks, n_a):
    j = pl.program_id(1)

    @pl.when(j == 0)
    def _():
        def body(q, carry):
            rows = pl.ds(pl.multiple_of(q * SSD_Q, SSD_Q), SSD_Q)
            h = _rms(x_ref[rows, :], g_ref[...]).astype(BF16)
            h_scr[rows, :] = h
            dt_ref[q] = lax.dot_general(wdt_ref[...], h, (((1,), (1,)), ((), ())),
                                        preferred_element_type=F32)
            return carry
        lax.fori_loop(0, n_chunks, body, 0, unroll=2)

    @pl.when(j < n_a)
    def _():
        o_ref[...] = jnp.dot(h_scr[...], wa_ref[...],
                             preferred_element_type=F32).astype(o_ref.dtype)

    @pl.when(j >= n_a)
    def _():
        o_ref[...] = jnp.dot(h_scr[...], wb_ref[...],
                             preferred_element_type=F32).astype(o_ref.dtype)


def _in_proj(x2d, g, w_a, w_b, w_dt_t, *, tm=1024, tn=1024):
    t, d = x2d.shape
    n_a = w_a.shape[1] // tn
    n_b = w_b.shape[1] // tn
    heads = w_dt_t.shape[0]
    n_chunks = tm // SSD_Q
    return pl.pallas_call(
        functools.partial(_in_proj_kernel, n_chunks=n_chunks, n_a=n_a),
        grid=(t // tm, n_a + n_b),
        in_specs=[
            pl.BlockSpec((tm, d), lambda i, j: (i, 0)),
            pl.BlockSpec((1, d), lambda i, j: (0, 0)),
            pl.BlockSpec((d, tn), lambda i, j: (0, jnp.minimum(j, n_a - 1))),
            pl.BlockSpec((d, tn), lambda i, j: (0, jnp.maximum(j - n_a, 0))),
            pl.BlockSpec((heads, d), lambda i, j: (0, 0)),
        ],
        out_specs=[
            pl.BlockSpec((tm, tn), lambda i, j: (i, j)),
            pl.BlockSpec((n_chunks, heads, SSD_Q), lambda i, j: (i, 0, 0)),
        ],
        out_shape=[
            jax.ShapeDtypeStruct((t, (n_a + n_b) * tn), BF16),
            jax.ShapeDtypeStruct((t // SSD_Q, heads, SSD_Q), F32),
        ],
        scratch_shapes=[pltpu.VMEM((tm, d), BF16)],
        compiler_params=_cparams(("parallel", "arbitrary"), 56),
        name="in_proj",
    )(x2d, g, w_a, w_b, w_dt_t)


def _ssd_kernel(z_ref, xs_ref, b_ref, c_ref, dt_ref, cw_ref, cb_ref, dtb_ref, alog_ref,
                dsk_ref, ng_ref, u3_ref, ex_ref, rep_ref, y_ref,
                state_scr, ubuf, xc, lhs_scr, cumj_scr, dl_scr, ci_scr, g_scr, s_scr,
                *, lc, gw):
    q_len = SSD_Q
    n_chunks = lc // q_len
    hpg = gw // SSM_HEAD_DIM
    ns_x = gw // LANES
    ns = ns_x + 2
    halo = 8
    pr = q_len // CONV_PHASES

    @pl.when(pl.program_id(2) == 0)
    def _():
        state_scr[...] = jnp.zeros_like(state_scr)
        ubuf[:, 0:halo, :] = jnp.zeros((ns, halo, LANES), F32)

    for s in range(ns_x):
        ubuf[s, halo:halo + lc, :] = xs_ref[:, s * LANES:(s + 1) * LANES].astype(F32)
    ubuf[ns_x, halo:halo + lc, :] = b_ref[...].astype(F32)
    ubuf[ns_x + 1, halo:halo + lc, :] = c_ref[...].astype(F32)

    dt_all = _softplus(dt_ref[...] + dtb_ref[...][None]).reshape(n_chunks * hpg, q_len)
    a_all = jnp.broadcast_to(-jnp.exp(alog_ref[...])[None],
                             (n_chunks, hpg, 1)).reshape(n_chunks * hpg, 1)
    da3 = jnp.concatenate(_split_terms(dt_all * a_all, 3), axis=1).astype(BF16)
    cum2 = jnp.dot(da3, u3_ref[...], preferred_element_type=F32) * LOG2_E
    cum2_last = cum2[:, q_len - 1:q_len]
    te_all = jnp.exp2(cum2_last - cum2) * dt_all
    cumj_scr[...] = (cum2 - jnp.log2(dt_all)).reshape(n_chunks, hpg, q_len)
    pieces = [p.reshape(n_chunks, hpg, q_len)
              for p in _split_terms(cum2, 3) + _split_terms(te_all, 2)]
    pieces.append(jnp.zeros((n_chunks, hpg, q_len), F32))
    lhs_scr[...] = jnp.concatenate(pieces, axis=1).astype(BF16)
    dl_scr[...] = jnp.sum(jnp.exp2(cum2_last).reshape(n_chunks, hpg, 1) * rep_ref[...][None],
                          axis=1)

    def conv(k):
        r0 = k * q_len if isinstance(k, int) else pl.multiple_of(k * q_len, q_len)
        for s in range(ns):
            lanes = slice(s * LANES, (s + 1) * LANES)
            taps = {d: ubuf.at[s][pl.ds(r0 + halo + d, pr, stride=CONV_PHASES), :]
                    for d in range(1 - CONV_WIDTH, CONV_PHASES)}
            for ph in range(CONV_PHASES):
                acc = cb_ref[:, lanes]
                for kk in range(CONV_WIDTH):
                    acc = acc + cw_ref[kk:kk + 1, lanes] * taps[ph - (CONV_WIDTH - 1) + kk]
                xc.at[s][pl.ds(r0 + ph, pr, stride=CONV_PHASES), :] = _silu_of_twice(acc)

    dsk = dsk_ref[...]
    ng = ng_ref[...]
    row_i = lax.broadcasted_iota(jnp.int32, (q_len, q_len), 0)
    col_j = lax.broadcasted_iota(jnp.int32, (q_len, q_len), 1)
    causal = row_i >= col_j
    low_half = lax.broadcasted_iota(jnp.int32, (1, LANES), 1) < SSM_HEAD_DIM

    def chunk_rows(k):
        start = k * q_len
        return pl.ds(start if isinstance(k, int) else pl.multiple_of(start, q_len), q_len)

    def load_x(rows):
        return jnp.concatenate([xc[s, rows, :] for s in range(ns_x)], axis=1)

    def prep(k, slot):
        rows = chunk_rows(k)
        xq = load_x(rows)
        cq = xc[ns_x + 1, rows, :]
        bq16 = xc[ns_x, rows, :].astype(BF16)
        cq16 = cq.astype(BF16)

        xp = lax.dot_general(lhs_scr[k], ex_ref[...], (((0,), (0,)), ((), ())),
                             preferred_element_type=F32)
        ci_scr[slot] = xp[:, 0:hpg * q_len]
        te_exp = xp[:, hpg * q_len:hpg * q_len + gw]

        cb = lax.dot_general(cq16, bq16, (((1,), (1,)), ((), ())),
                             preferred_element_type=F32)
        g_scr[slot] = jnp.concatenate([cb, cq], axis=1)

        xte = (xq * te_exp).astype(BF16)
        s_scr[slot] = lax.dot_general(bq16, xte, (((0,), (0,)), ((), ())),
                                      preferred_element_type=F32)

    def heads(k, slot):
        rows = chunk_rows(k)
        xq = load_x(rows)
        g_mat = g_scr[slot]
        cum_j = cumj_scr[k]
        st = state_scr[...]
        rhs = jnp.concatenate([xq.astype(BF16), st.astype(BF16)], axis=0)
        ys = []
        for m in range(hpg // 2):
            rp = rhs[:, m * LANES:(m + 1) * LANES]
            parts = []
            for r in (2 * m, 2 * m + 1):
                ci = ci_scr[slot, :, r * q_len:(r + 1) * q_len]
                seg = jnp.where(causal, ci - cum_j[r:r + 1, :], -jnp.inf)
                lhs = (g_mat * jnp.exp2(jnp.concatenate([seg, ci], axis=1))).astype(BF16)
                parts.append(jnp.dot(lhs, rp, preferred_element_type=F32))
            ys.append(jnp.where(low_half, parts[0], parts[1]))
        y_mm = jnp.concatenate(ys, axis=1)

        state_scr[...] = st * dl_scr[pl.ds(k, 1), :] + s_scr[slot]

        z_half = z_ref[rows, :].astype(F32)
        y = (y_mm + dsk * xq) * _silu_of_twice(z_half)
        y_ref[rows, :] = _rms(y, ng).astype(y_ref.dtype)

    conv(0)
    conv(1)
    prep(0, 0)

    def pair(kk, carry):
        k0 = 2 * kk
        heads(k0, 0)
        prep(k0 + 1, 1)
        conv(k0 + 2)
        heads(k0 + 1, 1)
        prep(k0 + 2, 0)
        conv(k0 + 3)
        return carry

    lax.fori_loop(0, n_chunks // 2 - 1, pair, 0)
    heads(n_chunks - 2, 0)
    prep(n_chunks - 1, 1)
    heads(n_chunks - 1, 1)
    ubuf[:, 0:halo, :] = ubuf[:, lc:lc + halo, :]


def _ssd(proj, dt_t, cw, cb, dtb, alog, dsk, ng, *, batch, seq, d_inner, lc=1024):
    t = proj.shape[0]
    groups = SSM_GROUPS
    gw = d_inner // groups
    hpg = gw // SSM_HEAD_DIM
    q_len = SSD_Q
    n_tiles = seq // lc
    n_chunks = lc // q_len
    cwid = gw + 2 * D_STATE
    ns = cwid // LANES
    assert D_STATE == q_len == LANES and hpg % 2 == 0 and gw % LANES == 0
    xs_blk0 = d_inner // gw
    b_blk0 = 2 * d_inner // D_STATE
    c_blk0 = b_blk0 + groups

    tri = np.triu(np.ones((q_len, q_len), np.float32))
    u3 = jnp.asarray(np.concatenate([tri, tri, tri], axis=0), BF16)
    rep = np.kron(np.eye(hpg, dtype=np.float32), np.ones((1, SSM_HEAD_DIM), np.float32))
    rq = np.kron(np.eye(hpg, dtype=np.float32), np.ones((1, q_len), np.float32))
    zq = np.zeros_like(rq)
    zr = np.zeros_like(rep)
    ex = np.concatenate([
        np.concatenate([rq, zr], axis=1), np.concatenate([rq, zr], axis=1),
        np.concatenate([rq, zr], axis=1),
        np.concatenate([zq, rep], axis=1), np.concatenate([zq, rep], axis=1),
        np.concatenate([zq, zr], axis=1),
    ], axis=0)
    ex = jnp.asarray(ex, BF16)
    rep = jnp.asarray(rep, F32)

    row = lambda b, g, c: b * n_tiles + c
    const2 = lambda b, g, c: (0, 0)
    per_group = lambda b, g, c: (g, 0, 0)
    return pl.pallas_call(
        functools.partial(_ssd_kernel, lc=lc, gw=gw),
        grid=(batch, groups, n_tiles),
        in_specs=[
            pl.BlockSpec((lc, gw), lambda b, g, c: (row(b, g, c), g)),
            pl.BlockSpec((lc, gw), lambda b, g, c: (row(b, g, c), xs_blk0 + g)),
            pl.BlockSpec((lc, D_STATE), lambda b, g, c: (row(b, g, c), b_blk0 + g)),
            pl.BlockSpec((lc, D_STATE), lambda b, g, c: (row(b, g, c), c_blk0 + g)),
            pl.BlockSpec((lc // q_len, hpg, q_len), lambda b, g, c: (row(b, g, c), g, 0)),
            pl.BlockSpec((None, CONV_WIDTH, cwid), per_group),
            pl.BlockSpec((None, 1, cwid), per_group),
            pl.BlockSpec((None, hpg, 1), per_group),
            pl.BlockSpec((None, hpg, 1), per_group),
            pl.BlockSpec((None, 1, gw), per_group),
            pl.BlockSpec((None, 1, gw), per_group),
            pl.BlockSpec(u3.shape, const2),
            pl.BlockSpec(ex.shape, const2),
            pl.BlockSpec(rep.shape, const2),
        ],
        out_specs=pl.BlockSpec((lc, gw), lambda b, g, c: (row(b, g, c), g)),
        out_shape=jax.ShapeDtypeStruct((t, d_inner), BF16),
        scratch_shapes=[
            pltpu.VMEM((D_STATE, gw), F32),
            pltpu.VMEM((ns, lc + 8, LANES), F32),
            pltpu.VMEM((ns, lc, LANES), F32),
            pltpu.VMEM((n_chunks, 6 * hpg, q_len), BF16),
            pltpu.VMEM((n_chunks, hpg, q_len), F32),
            pltpu.VMEM((n_chunks, gw), F32),
            pltpu.VMEM((2, q_len, hpg * q_len), F32),
            pltpu.VMEM((2, q_len, q_len + D_STATE), F32),
            pltpu.VMEM((2, D_STATE, gw), F32),
        ],
        compiler_params=_cparams(("parallel", "parallel", "arbitrary"), 32),
        name="ssd",
    )(proj, proj, proj, proj, dt_t, cw, cb, dtb, alog, dsk, ng, u3, ex, rep)


def _pool_kernel(u_ref, halo_ref, wg_ref, sc_ref, o_ref, buf, *, tp, tiles_per_seq, gdim):
    i = pl.program_id(0)
    seq_tile = i % tiles_per_seq
    keep = (seq_tile != 0).astype(F32)
    buf[0:POOL_HALO, :] = halo_ref[...].astype(F32) * keep
    buf[POOL_HALO:POOL_HALO + tp, :] = u_ref[...].astype(F32)

    rc = 128
    for r0 in range(0, tp, rc):
        pos = (seq_tile * tp + r0 + 1
               + lax.broadcasted_iota(jnp.int32, (rc, 1), 0)).astype(F32)
        for gi, win in enumerate(POOL_WINDOWS):
            cols = slice(gi * gdim, (gi + 1) * gdim)
            base = POOL_HALO + r0
            cur = buf[base:base + rc, cols]
            s = cur
            for k in range(1, win):
                s = s + buf[base - k:base - k + rc, cols]
            pooled = (s / jnp.minimum(pos, float(win)) - cur).astype(BF16)
            mixed = jnp.dot(pooled, wg_ref[gi], preferred_element_type=F32)
            o_ref[r0:r0 + rc, cols] = (mixed * sc_ref[:, cols]).astype(o_ref.dtype)


def _pool(proj, w_grp, scale, *, seq, d_pool, col0, tp=512):
    t = proj.shape[0]
    n_groups = len(POOL_WINDOWS)
    gdim = d_pool // n_groups
    tiles_per_seq = seq // tp
    cblk = col0 // d_pool
    hb = tp // POOL_HALO
    return pl.pallas_call(
        functools.partial(_pool_kernel, tp=tp, tiles_per_seq=tiles_per_seq, gdim=gdim),
        grid=(t // tp,),
        in_specs=[
            pl.BlockSpec((tp, d_pool), lambda i: (i, cblk)),
            pl.BlockSpec((POOL_HALO, d_pool), lambda i: (jnp.maximum(i * hb - 1, 0), cblk)),
            pl.BlockSpec((n_groups, gdim, gdim), lambda i: (0, 0, 0)),
            pl.BlockSpec((1, d_pool), lambda i: (0, 0)),
        ],
        out_specs=pl.BlockSpec((tp, d_pool), lambda i: (i, 0)),
        out_shape=jax.ShapeDtypeStruct((t, d_pool), BF16),
        scratch_shapes=[pltpu.VMEM((tp + POOL_HALO, d_pool), F32)],
        compiler_params=_cparams(("parallel",), 32),
        name="pool",
    )(proj, proj, w_grp, scale)


def _merge_kernel(ys_ref, yp_ref, gs_ref, gp_ref, ws_ref, wp_ref, o_ref):
    a = jnp.dot(ys_ref[...], ws_ref[...], preferred_element_type=F32)
    b = jnp.dot(yp_ref[...], wp_ref[...], preferred_element_type=F32)
    merged = (jax.nn.sigmoid(gs_ref[...].astype(F32)) * a
              + jax.nn.sigmoid(gp_ref[...].astype(F32)) * b)
    o_ref[...] = merged.astype(o_ref.dtype)


def _merge(y_ssm, y_pool, proj, w_so, w_po, *, gs_col0, gp_col0, tm=1024, tn=512):
    t, d_inner = y_ssm.shape
    d_pool = y_pool.shape[1]
    d = w_so.shape[1]
    gs_blk = gs_col0 // tn
    gp_blk = gp_col0 // tn
    return pl.pallas_call(
        _merge_kernel,
        grid=(t // tm, d // tn),
        in_specs=[
            pl.BlockSpec((tm, d_inner), lambda i, j: (i, 0)),
            pl.BlockSpec((tm, d_pool), lambda i, j: (i, 0)),
            pl.BlockSpec((tm, tn), lambda i, j: (i, gs_blk + j)),
            pl.BlockSpec((tm, tn), lambda i, j: (i, gp_blk + j)),
            pl.BlockSpec((d_inner, tn), lambda i, j: (0, j)),
            pl.BlockSpec((d_pool, tn), lambda i, j: (0, j)),
        ],
        out_specs=pl.BlockSpec((tm, tn), lambda i, j: (i, j)),
        out_shape=jax.ShapeDtypeStruct((t, d), BF16),
        compiler_params=_cparams(("parallel", "arbitrary"), 48),
        name="merge",
    )(y_ssm, y_pool, proj, proj, w_so, w_po)


def _mix_kernel(m_ref, x_ref, w_ref, gpost_ref, gpre_ref, x1_ref, h2_ref):
    for r0 in range(0, m_ref.shape[0], ROW_SUB):
        rows = slice(r0, r0 + ROW_SUB)
        mix = jnp.dot(m_ref[rows, :], w_ref[...], preferred_element_type=F32)
        x1 = x_ref[rows, :] + _rms(mix, gpost_ref[...])
        x1_ref[rows, :] = x1
        h2_ref[rows, :] = _rms(x1, gpre_ref[...]).astype(h2_ref.dtype)


def _mix(merged, x2d, w_mix, g_post, g_pre, *, tm=512):
    t, d = x2d.shape
    return pl.pallas_call(
        _mix_kernel,
        grid=(t // tm,),
        in_specs=[
            pl.BlockSpec((tm, d), lambda i: (i, 0)),
            pl.BlockSpec((tm, d), lambda i: (i, 0)),
            pl.BlockSpec((d, d), lambda i: (0, 0)),
            pl.BlockSpec((1, d), lambda i: (0, 0)),
            pl.BlockSpec((1, d), lambda i: (0, 0)),
        ],
        out_specs=[
            pl.BlockSpec((tm, d), lambda i: (i, 0)),
            pl.BlockSpec((tm, d), lambda i: (i, 0)),
        ],
        out_shape=[
            jax.ShapeDtypeStruct((t, d), F32),
            jax.ShapeDtypeStruct((t, d), BF16),
        ],
        compiler_params=_cparams(("parallel",), 48),
        name="mix",
    )(merged, x2d, w_mix, g_post, g_pre)


def _ffn_up_kernel(h_ref, wg_ref, wu_ref, o_ref):
    h = h_ref[...]
    gate = jnp.dot(h, wg_ref[...], preferred_element_type=F32)
    up = jnp.dot(h, wu_ref[...], preferred_element_type=F32)
    o_ref[...] = (_silu(gate) * up).astype(o_ref.dtype)


def _ffn_up(h2, w_ffn_in, *, tm=1024, tf=512):
    t, d = h2.shape
    d_ff = w_ffn_in.shape[1] // 2
    n_f = d_ff // tf
    return pl.pallas_call(
        _ffn_up_kernel,
        grid=(t // tm, n_f),
        in_specs=[
            pl.BlockSpec((tm, d), lambda i, j: (i, 0)),
            pl.BlockSpec((d, tf), lambda i, j: (0, j)),
            pl.BlockSpec((d, tf), lambda i, j: (0, n_f + j)),
        ],
        out_specs=pl.BlockSpec((tm, tf), lambda i, j: (i, j)),
        out_shape=jax.ShapeDtypeStruct((t, d_ff), BF16),
        compiler_params=_cparams(("parallel", "arbitrary"), 40),
        name="ffn_up",
    )(h2, w_ffn_in, w_ffn_in)


def _ffn_down_kernel(a_ref, wo_ref, o_ref):
    o_ref[...] = jnp.dot(a_ref[...], wo_ref[...], preferred_element_type=F32)


def _ffn_down(act, w_out, *, tm=1024, tn=512):
    t, d_ff = act.shape
    d = w_out.shape[1]
    return pl.pallas_call(
        _ffn_down_kernel,
        grid=(t // tm, d // tn),
        in_specs=[
            pl.BlockSpec((tm, d_ff), lambda i, j: (i, 0)),
            pl.BlockSpec((d_ff, tn), lambda i, j: (0, j)),
        ],
        out_specs=pl.BlockSpec((tm, tn), lambda i, j: (i, j)),
        out_shape=jax.ShapeDtypeStruct((t, d), F32),
        compiler_params=_cparams(("parallel", "arbitrary"), 48),
        name="ffn_down",
    )(act, w_out)


def _ple_kernel(f_ref, x1_ref, p_ref, g_ref, wg_ref, wp_ref, o_ref):
    for r0 in range(0, f_ref.shape[0], ROW_SUB):
        rows = slice(r0, r0 + ROW_SUB)
        x2 = x1_ref[rows, :] + _rms(f_ref[rows, :], g_ref[...])
        gate = jnp.dot(x2.astype(BF16), wg_ref[...], preferred_element_type=F32)
        ple = jnp.dot(p_ref[rows, :].astype(BF16), wp_ref[...], preferred_element_type=F32)
        o_ref[rows, :] = x2 + jax.nn.sigmoid(gate) * ple


def _ple(f, x1, p2d, g_post, w_gate, w_proj, *, tm=512):
    t, d = x1.shape
    k = p2d.shape[1]
    return pl.pallas_call(
        _ple_kernel,
        grid=(t // tm,),
        in_specs=[
            pl.BlockSpec((tm, d), lambda i: (i, 0)),
            pl.BlockSpec((tm, d), lambda i: (i, 0)),
            pl.BlockSpec((tm, k), lambda i: (i, 0)),
            pl.BlockSpec((1, d), lambda i: (0, 0)),
            pl.BlockSpec((d, d), lambda i: (0, 0)),
            pl.BlockSpec((k, d), lambda i: (0, 0)),
        ],
        out_specs=pl.BlockSpec((tm, d), lambda i: (i, 0)),
        out_shape=jax.ShapeDtypeStruct((t, d), F32),
        compiler_params=_cparams(("parallel",), 52),
        name="ple",
    )(f, x1, p2d, g_post, w_gate, w_proj)


def kernel(x, p, w_in, conv_w, conv_b, dt_bias, a_log, d_skip, ssm_norm_g, w_ssm_out,
           w_pool_grp, pool_scale, w_pool_out, w_mix_out, g_pre_mix, g_post_mix,
           g_pre_ffn, g_post_ffn, w_ffn_in, w_ffn_out, w_ple_proj, w_ple_gate):
    batch, seq, d = x.shape
    depth = w_in.shape[0]
    d_inner = w_ssm_out.shape[1]
    d_pool = w_pool_out.shape[1]
    heads = dt_bias.shape[1]
    groups = SSM_GROUPS
    hpg = heads // groups
    gw = d_inner // groups
    n_bc = groups * D_STATE
    conv_dim = d_inner + 2 * n_bc
    t = batch * seq
    assert heads * SSM_HEAD_DIM == d_inner and conv_w.shape[2] == conv_dim
    assert w_in.shape[2] == d_inner + conv_dim + heads + d_pool + 2 * d

    dt_col0 = d_inner + conv_dim
    pool_col0 = dt_col0
    gs_col0 = pool_col0 + d_pool
    gp_col0 = gs_col0 + d

    x2d = x.reshape(t, d)
    for li in range(depth):
        wi = w_in[li]
        z_half = jnp.where(jnp.arange(dt_col0) < d_inner, 0.5, 1.0).astype(F32)
        w_a = (wi[:, :dt_col0] * z_half).astype(BF16)
        w_b = wi[:, dt_col0 + heads:].astype(BF16)
        w_dt_t = wi[:, dt_col0:dt_col0 + heads].T.astype(BF16)

        def per_group_cols(v):
            xs = v[..., :d_inner].reshape(v.shape[:-1] + (groups, gw))
            bm = v[..., d_inner:d_inner + n_bc].reshape(v.shape[:-1] + (groups, D_STATE))
            cm = v[..., d_inner + n_bc:].reshape(v.shape[:-1] + (groups, D_STATE))
            cat = jnp.concatenate([xs, bm, cm], axis=-1)
            return jnp.moveaxis(cat, -2, 0)
        cw = per_group_cols(0.5 * conv_w[li])
        cb = per_group_cols(0.5 * conv_b[li][None, :])
        dtb = dt_bias[li].reshape(groups, hpg, 1)
        alog = a_log[li].reshape(groups, hpg, 1)
        dsk = jnp.repeat(d_skip[li], SSM_HEAD_DIM).reshape(groups, 1, gw)
        ng = ssm_norm_g[li].reshape(groups, 1, gw)

        proj, dt_t = _in_proj(x2d, g_pre_mix[li][None, :], w_a, w_b, w_dt_t)
        y_ssm = _ssd(proj, dt_t, cw, cb, dtb, alog, dsk, ng,
                     batch=batch, seq=seq, d_inner=d_inner)
        y_pool = _pool(proj, w_pool_grp[li].astype(BF16), pool_scale[li][None, :],
                       seq=seq, d_pool=d_pool, col0=pool_col0)
        merged = _merge(y_ssm, y_pool, proj, w_ssm_out[li].astype(BF16),
                        w_pool_out[li].astype(BF16), gs_col0=gs_col0, gp_col0=gp_col0)
        x1, h2 = _mix(merged, x2d, w_mix_out[li].astype(BF16),
                      g_post_mix[li][None, :], g_pre_ffn[li][None, :])
        act = _ffn_up(h2, w_ffn_in[li].astype(BF16))
        f = _ffn_down(act, w_ffn_out[li].astype(BF16))
        x2d = _ple(f, x1, p[li].reshape(t, -1), g_post_ffn[li][None, :],
                   w_ple_gate[li].astype(BF16), w_ple_proj[li].astype(BF16))
    return x2d.reshape(batch, seq, d)
```

```python
import functools

import jax
import jax.numpy as jnp
import numpy as np
from jax import lax
from jax.experimental import pallas as pl
from jax.experimental.pallas import tpu as pltpu

F32 = jnp.float32
BF16 = jnp.bfloat16

EPS = 1e-6
SSM_HEAD_DIM = 64
SSM_GROUPS = 8
D_STATE = 128
CONV_WIDTH = 4
POOL_WINDOWS = (2, 4, 8, 16)
POOL_HALO = 16
POOL_ROWS = 128
SSD_Q = 128
LANES = 128
CONV_PHASES = 4
ROW_SUB = 256

LOG2_E = 1.4426950408889634

MIB = 1024 * 1024


def _cparams(semantics, vmem_mib):
    return pltpu.CompilerParams(dimension_semantics=semantics,
                                vmem_limit_bytes=vmem_mib * MIB)


def _rms(v, g):
    ms = jnp.mean(v * v, axis=-1, keepdims=True)
    return v * lax.rsqrt(ms + EPS) * g


def _silu(v):
    return v * jax.nn.sigmoid(v)


def _silu_of_twice(h):
    return h + h * jnp.tanh(h)


def _softplus(v):
    return jnp.maximum(v, 0.0) + jnp.log1p(jnp.exp(-jnp.abs(v)))


def _split_terms(v, n_terms):
    pieces = []
    rem = v
    for _ in range(n_terms - 1):
        hi = rem.astype(BF16).astype(F32)
        pieces.append(hi)
        rem = rem - hi
    pieces.append(rem)
    return pieces


def _in_proj_kernel(x_ref, g_ref, w_ref, wdt_ref, o_ref, dt_ref, h_scr, *, n_chunks):
    @pl.when(pl.program_id(1) == 0)
    def _():
        def body(q, carry):
            rows = pl.ds(pl.multiple_of(q * SSD_Q, SSD_Q), SSD_Q)
            h = _rms(x_ref[rows, :], g_ref[...]).astype(BF16)
            h_scr[rows, :] = h
            dt_ref[q] = lax.dot_general(wdt_ref[...], h, (((0,), (1,)), ((), ())),
                                        preferred_element_type=F32)
            return carry
        lax.fori_loop(0, n_chunks, body, 0, unroll=2)

    o_ref[...] = jnp.dot(h_scr[...], w_ref[...],
                         preferred_element_type=F32).astype(o_ref.dtype)


def _in_proj(x2d, g, w_main, w_dt, *, tm=1024, tn=1024):
    t, d = x2d.shape
    n = w_main.shape[1]
    heads = w_dt.shape[1]
    n_chunks = tm // SSD_Q
    return pl.pallas_call(
        functools.partial(_in_proj_kernel, n_chunks=n_chunks),
        grid=(t // tm, n // tn),
        in_specs=[
            pl.BlockSpec((tm, d), lambda i, j: (i, 0)),
            pl.BlockSpec((1, d), lambda i, j: (0, 0)),
            pl.BlockSpec((d, tn), lambda i, j: (0, j)),
            pl.BlockSpec((d, heads), lambda i, j: (0, 0)),
        ],
        out_specs=[
            pl.BlockSpec((tm, tn), lambda i, j: (i, j)),
            pl.BlockSpec((n_chunks, heads, SSD_Q), lambda i, j: (i, 0, 0)),
        ],
        out_shape=[
            jax.ShapeDtypeStruct((t, n), BF16),
            jax.ShapeDtypeStruct((t // SSD_Q, heads, SSD_Q), F32),
        ],
        scratch_shapes=[pltpu.VMEM((tm, d), BF16)],
        compiler_params=_cparams(("parallel", "arbitrary"), 48),
        name="in_proj",
    )(x2d, g, w_main, w_dt)


def _ssd_kernel(z_ref, xs_ref, b_ref, c_ref, dt_ref, cw_ref, cb_ref, dtb_ref, alog_ref,
                dsk_ref, ng_ref, u3_ref, ex_ref, rep_ref, y_ref,
                state_scr, ubuf, xc, lhs_scr, cumj_scr, dl_scr, ci_scr, g_scr, s_scr,
                *, lc, gw):
    q_len = SSD_Q
    n_chunks = lc // q_len
    hpg = gw // SSM_HEAD_DIM
    ns_x = gw // LANES
    ns = ns_x + 2
    halo = 8
    pr = q_len // CONV_PHASES

    @pl.when(pl.program_id(2) == 0)
    def _():
        state_scr[...] = jnp.zeros_like(state_scr)
        ubuf[:, 0:halo, :] = jnp.zeros((ns, halo, LANES), F32)

    for s in range(ns_x):
        ubuf[s, halo:halo + lc, :] = xs_ref[:, s * LANES:(s + 1) * LANES].astype(F32)
    ubuf[ns_x, halo:halo + lc, :] = b_ref[...].astype(F32)
    ubuf[ns_x + 1, halo:halo + lc, :] = c_ref[...].astype(F32)

    dt_all = _softplus(dt_ref[...] + dtb_ref[...][None]).reshape(n_chunks * hpg, q_len)
    a_all = jnp.broadcast_to(-jnp.exp(alog_ref[...])[None],
                             (n_chunks, hpg, 1)).reshape(n_chunks * hpg, 1)
    da3 = jnp.concatenate(_split_terms(dt_all * a_all, 3), axis=1).astype(BF16)
    cum2 = jnp.dot(da3, u3_ref[...], preferred_element_type=F32) * LOG2_E
    cum2_last = cum2[:, q_len - 1:q_len]
    te_all = jnp.exp2(cum2_last - cum2) * dt_all
    cumj_scr[...] = (cum2 - jnp.log2(dt_all)).reshape(n_chunks, hpg, q_len)
    pieces = [p.reshape(n_chunks, hpg, q_len)
              for p in _split_terms(cum2, 3) + _split_terms(te_all, 2)]
    pieces.append(jnp.zeros((n_chunks, hpg, q_len), F32))
    lhs_scr[...] = jnp.concatenate(pieces, axis=1).astype(BF16)
    dl_scr[...] = jnp.sum(jnp.exp2(cum2_last).reshape(n_chunks, hpg, 1) * rep_ref[...][None],
                          axis=1)

    def conv(k):
        r0 = k * q_len if isinstance(k, int) else pl.multiple_of(k * q_len, q_len)
        for s in range(ns):
            lanes = slice(s * LANES, (s + 1) * LANES)
            taps = {d: ubuf.at[s][pl.ds(r0 + halo + d, pr, stride=CONV_PHASES), :]
                    for d in range(1 - CONV_WIDTH, CONV_PHASES)}
            for ph in range(CONV_PHASES):
                acc = cb_ref[:, lanes]
                for kk in range(CONV_WIDTH):
                    acc = acc + cw_ref[kk:kk + 1, lanes] * taps[ph - (CONV_WIDTH - 1) + kk]
                xc.at[s][pl.ds(r0 + ph, pr, stride=CONV_PHASES), :] = _silu_of_twice(acc)

    dsk = dsk_ref[...]
    ng = ng_ref[...]
    row_i = lax.broadcasted_iota(jnp.int32, (q_len, q_len), 0)
    col_j = lax.broadcasted_iota(jnp.int32, (q_len, q_len), 1)
    causal = row_i >= col_j
    low_half = lax.broadcasted_iota(jnp.int32, (1, LANES), 1) < SSM_HEAD_DIM

    def chunk_rows(k):
        start = k * q_len
        return pl.ds(start if isinstance(k, int) else pl.multiple_of(start, q_len), q_len)

    def load_x(rows):
        return jnp.concatenate([xc[s, rows, :] for s in range(ns_x)], axis=1)

    def prep(k, slot):
        rows = chunk_rows(k)
        xq = load_x(rows)
        cq = xc[ns_x + 1, rows, :]
        bq16 = xc[ns_x, rows, :].astype(BF16)
        cq16 = cq.astype(BF16)

        xp = lax.dot_general(lhs_scr[k], ex_ref[...], (((0,), (0,)), ((), ())),
                             preferred_element_type=F32)
        ci_scr[slot] = xp[:, 0:hpg * q_len]
        te_exp = xp[:, hpg * q_len:hpg * q_len + gw]

        cb = lax.dot_general(cq16, bq16, (((1,), (1,)), ((), ())),
                             preferred_element_type=F32)
        g_scr[slot] = jnp.concatenate([cb, cq], axis=1)

        xte = (xq * te_exp).astype(BF16)
        s_scr[slot] = lax.dot_general(bq16, xte, (((0,), (0,)), ((), ())),
                                      preferred_element_type=F32)

    def heads(k, slot):
        rows = chunk_rows(k)
        xq = load_x(rows)
        g_mat = g_scr[slot]
        cum_j = cumj_scr[k]
        st = state_scr[...]
        rhs = jnp.concatenate([xq.astype(BF16), st.astype(BF16)], axis=0)
        ys = []
        for m in range(hpg // 2):
            rp = rhs[:, m * LANES:(m + 1) * LANES]
            parts = []
            for r in (2 * m, 2 * m + 1):
                ci = ci_scr[slot, :, r * q_len:(r + 1) * q_len]
                seg = jnp.where(causal, ci - cum_j[r:r + 1, :], -jnp.inf)
                lhs = (g_mat * jnp.exp2(jnp.concatenate([seg, ci], axis=1))).astype(BF16)
                parts.append(jnp.dot(lhs, rp, preferred_element_type=F32))
            ys.append(jnp.where(low_half, parts[0], parts[1]))
        y_mm = jnp.concatenate(ys, axis=1)

        state_scr[...] = st * dl_scr[pl.ds(k, 1), :] + s_scr[slot]

        z_half = z_ref[rows, :].astype(F32)
        y = (y_mm + dsk * xq) * _silu_of_twice(z_half)
        y_ref[rows, :] = _rms(y, ng).astype(y_ref.dtype)

    conv(0)
    conv(1)
    prep(0, 0)

    def pair(kk, carry):
        k0 = 2 * kk
        heads(k0, 0)
        prep(k0 + 1, 1)
        conv(k0 + 2)
        heads(k0 + 1, 1)
        prep(k0 + 2, 0)
        conv(k0 + 3)
        return carry

    lax.fori_loop(0, n_chunks // 2 - 1, pair, 0)
    heads(n_chunks - 2, 0)
    prep(n_chunks - 1, 1)
    heads(n_chunks - 1, 1)
    ubuf[:, 0:halo, :] = ubuf[:, lc:lc + halo, :]


def _ssd(proj, dt_t, cw, cb, dtb, alog, dsk, ng, *, batch, seq, d_inner, lc=2048):
    t = proj.shape[0]
    groups = SSM_GROUPS
    gw = d_inner // groups
    hpg = gw // SSM_HEAD_DIM
    q_len = SSD_Q
    n_tiles = seq // lc
    n_chunks = lc // q_len
    cwid = gw + 2 * D_STATE
    ns = cwid // LANES
    assert D_STATE == q_len == LANES and hpg % 2 == 0 and gw % LANES == 0
    xs_blk0 = d_inner // gw
    b_blk0 = 2 * d_inner // D_STATE
    c_blk0 = b_blk0 + groups

    tri = np.triu(np.ones((q_len, q_len), np.float32))
    u3 = jnp.asarray(np.concatenate([tri, tri, tri], axis=0), BF16)
    rep = np.kron(np.eye(hpg, dtype=np.float32), np.ones((1, SSM_HEAD_DIM), np.float32))
    rq = np.kron(np.eye(hpg, dtype=np.float32), np.ones((1, q_len), np.float32))
    zq = np.zeros_like(rq)
    zr = np.zeros_like(rep)
    ex = np.concatenate([
        np.concatenate([rq, zr], axis=1), np.concatenate([rq, zr], axis=1),
        np.concatenate([rq, zr], axis=1),
        np.concatenate([zq, rep], axis=1), np.concatenate([zq, rep], axis=1),
        np.concatenate([zq, zr], axis=1),
    ], axis=0)
    ex = jnp.asarray(ex, BF16)
    rep = jnp.asarray(rep, F32)

    row = lambda b, g, c: b * n_tiles + c
    const2 = lambda b, g, c: (0, 0)
    per_group = lambda b, g, c: (g, 0, 0)
    return pl.pallas_call(
        functools.partial(_ssd_kernel, lc=lc, gw=gw),
        grid=(batch, groups, n_tiles),
        in_specs=[
            pl.BlockSpec((lc, gw), lambda b, g, c: (row(b, g, c), g)),
            pl.BlockSpec((lc, gw), lambda b, g, c: (row(b, g, c), xs_blk0 + g)),
            pl.BlockSpec((lc, D_STATE), lambda b, g, c: (row(b, g, c), b_blk0 + g)),
            pl.BlockSpec((lc, D_STATE), lambda b, g, c: (row(b, g, c), c_blk0 + g)),
            pl.BlockSpec((lc // q_len, hpg, q_len), lambda b, g, c: (row(b, g, c), g, 0)),
            pl.BlockSpec((None, CONV_WIDTH, cwid), per_group),
            pl.BlockSpec((None, 1, cwid), per_group),
            pl.BlockSpec((None, hpg, 1), per_group),
            pl.BlockSpec((None, hpg, 1), per_group),
            pl.BlockSpec((None, 1, gw), per_group),
            pl.BlockSpec((None, 1, gw), per_group),
            pl.BlockSpec(u3.shape, const2),
            pl.BlockSpec(ex.shape, const2),
            pl.BlockSpec(rep.shape, const2),
        ],
        out_specs=pl.BlockSpec((lc, gw), lambda b, g, c: (row(b, g, c), g)),
        out_shape=jax.ShapeDtypeStruct((t, d_inner), BF16),
        scratch_shapes=[
            pltpu.VMEM((D_STATE, gw), F32),
            pltpu.VMEM((ns, lc + 8, LANES), F32),
            pltpu.VMEM((ns, lc, LANES), F32),
            pltpu.VMEM((n_chunks, 6 * hpg, q_len), BF16),
            pltpu.VMEM((n_chunks, hpg, q_len), F32),
            pltpu.VMEM((n_chunks, gw), F32),
            pltpu.VMEM((2, q_len, hpg * q_len), F32),
            pltpu.VMEM((2, q_len, q_len + D_STATE), F32),
            pltpu.VMEM((2, D_STATE, gw), F32),
        ],
        compiler_params=_cparams(("parallel", "parallel", "arbitrary"), 44),
        name="ssd",
    )(proj, proj, proj, proj, dt_t, cw, cb, dtb, alog, dsk, ng, u3, ex, rep)


def _pool_kernel(u_ref, halo_ref, band_ref, wg_ref, sc_ref, o_ref, buf, *, tp, tiles_per_seq, gdim):
    i = pl.program_id(0)
    seq_tile = i % tiles_per_seq
    keep = (seq_tile != 0).astype(u_ref.dtype)
    buf[0:POOL_HALO, :] = halo_ref[...] * keep
    buf[POOL_HALO:POOL_HALO + tp, :] = u_ref[...]

    rc = POOL_ROWS
    for r0 in range(0, tp, rc):
        pos = (seq_tile * tp + r0 + 1
               + lax.broadcasted_iota(jnp.int32, (rc, LANES), 0)).astype(F32)
        for gi, win in enumerate(POOL_WINDOWS):
            cols = slice(gi * gdim, (gi + 1) * gdim)
            ext = buf[r0:r0 + rc + POOL_HALO, cols]
            wsum = jnp.dot(band_ref[gi], ext, preferred_element_type=F32)
            inv = 1.0 / jnp.minimum(pos, float(win))
            inv = jnp.concatenate([inv] * (gdim // LANES), axis=1)
            cur = u_ref[r0:r0 + rc, cols].astype(F32)
            pooled = (wsum * inv - cur).astype(BF16)
            mixed = jnp.dot(pooled, wg_ref[gi], preferred_element_type=F32)
            o_ref[r0:r0 + rc, cols] = (mixed * sc_ref[:, cols]).astype(o_ref.dtype)


def _pool(proj, w_grp, scale, *, seq, d_pool, col0, tp=512):
    t = proj.shape[0]
    n_groups = len(POOL_WINDOWS)
    gdim = d_pool // n_groups
    tiles_per_seq = seq // tp
    cblk = col0 // d_pool
    hb = tp // POOL_HALO
    r_idx = np.arange(POOL_ROWS)[:, None] + POOL_HALO
    c_idx = np.arange(POOL_ROWS + POOL_HALO)[None, :]
    band = np.stack([((c_idx <= r_idx) & (c_idx > r_idx - w)).astype(np.float32)
                     for w in POOL_WINDOWS])
    band = jnp.asarray(band, BF16)
    return pl.pallas_call(
        functools.partial(_pool_kernel, tp=tp, tiles_per_seq=tiles_per_seq, gdim=gdim),
        grid=(t // tp,),
        in_specs=[
            pl.BlockSpec((tp, d_pool), lambda i: (i, cblk)),
            pl.BlockSpec((POOL_HALO, d_pool), lambda i: (jnp.maximum(i * hb - 1, 0), cblk)),
            pl.BlockSpec(band.shape, lambda i: (0, 0, 0)),
            pl.BlockSpec((n_groups, gdim, gdim), lambda i: (0, 0, 0)),
            pl.BlockSpec((1, d_pool), lambda i: (0, 0)),
        ],
        out_specs=pl.BlockSpec((tp, d_pool), lambda i: (i, 0)),
        out_shape=jax.ShapeDtypeStruct((t, d_pool), BF16),
        scratch_shapes=[pltpu.VMEM((tp + POOL_HALO, d_pool), BF16)],
        compiler_params=_cparams(("parallel",), 32),
        name="pool",
    )(proj, proj, band, w_grp, scale)


def _merge_kernel(ys_ref, yp_ref, gs_ref, gp_ref, ws_ref, wp_ref, o_ref):
    a = jnp.dot(ys_ref[...], ws_ref[...], preferred_element_type=F32)
    b = jnp.dot(yp_ref[...], wp_ref[...], preferred_element_type=F32)
    merged = (jax.nn.sigmoid(gs_ref[...].astype(F32)) * a
              + jax.nn.sigmoid(gp_ref[...].astype(F32)) * b)
    o_ref[...] = merged.astype(o_ref.dtype)


def _merge(y_ssm, y_pool, proj, w_so, w_po, *, gs_col0, gp_col0, tm=1024, tn=512):
    t, d_inner = y_ssm.shape
    d_pool = y_pool.shape[1]
    d = w_so.shape[1]
    gs_blk = gs_col0 // tn
    gp_blk = gp_col0 // tn
    return pl.pallas_call(
        _merge_kernel,
        grid=(t // tm, d // tn),
        in_specs=[
            pl.BlockSpec((tm, d_inner), lambda i, j: (i, 0)),
            pl.BlockSpec((tm, d_pool), lambda i, j: (i, 0)),
            pl.BlockSpec((tm, tn), lambda i, j: (i, gs_blk + j)),
            pl.BlockSpec((tm, tn), lambda i, j: (i, gp_blk + j)),
            pl.BlockSpec((d_inner, tn), lambda i, j: (0, j)),
            pl.BlockSpec((d_pool, tn), lambda i, j: (0, j)),
        ],
        out_specs=pl.BlockSpec((tm, tn), lambda i, j: (i, j)),
        out_shape=jax.ShapeDtypeStruct((t, d), BF16),
        compiler_params=_cparams(("parallel", "arbitrary"), 48),
        name="merge",
    )(y_ssm, y_pool, proj, proj, w_so, w_po)


def _mix_kernel(m_ref, x_ref, w_ref, gpost_ref, gpre_ref, x1_ref, h2_ref):
    for r0 in range(0, m_ref.shape[0], ROW_SUB):
        rows = slice(r0, r0 + ROW_SUB)
        mix = jnp.dot(m_ref[rows, :], w_ref[...], preferred_element_type=F32)
        x1 = x_ref[rows, :] + _rms(mix, gpost_ref[...])
        x1_ref[rows, :] = x1
        h2_ref[rows, :] = _rms(x1, gpre_ref[...]).astype(h2_ref.dtype)


def _mix(merged, x2d, w_mix, g_post, g_pre, *, tm=512):
    t, d = x2d.shape
    return pl.pallas_call(
        _mix_kernel,
        grid=(t // tm,),
        in_specs=[
            pl.BlockSpec((tm, d), lambda i: (i, 0)),
            pl.BlockSpec((tm, d), lambda i: (i, 0)),
            pl.BlockSpec((d, d), lambda i: (0, 0)),
            pl.BlockSpec((1, d), lambda i: (0, 0)),
            pl.BlockSpec((1, d), lambda i: (0, 0)),
        ],
        out_specs=[
            pl.BlockSpec((tm, d), lambda i: (i, 0)),
            pl.BlockSpec((tm, d), lambda i: (i, 0)),
        ],
        out_shape=[
            jax.ShapeDtypeStruct((t, d), F32),
            jax.ShapeDtypeStruct((t, d), BF16),
        ],
        compiler_params=_cparams(("parallel",), 48),
        name="mix",
    )(merged, x2d, w_mix, g_post, g_pre)


def _ffn_up_kernel(h_ref, wg_ref, wu_ref, o_ref):
    h = h_ref[...]
    gate = jnp.dot(h, wg_ref[...], preferred_element_type=F32)
    up = jnp.dot(h, wu_ref[...], preferred_element_type=F32)
    o_ref[...] = (_silu(gate) * up).astype(o_ref.dtype)


def _ffn_up(h2, w_ffn_in, *, tm=1024, tf=512):
    t, d = h2.shape
    d_ff = w_ffn_in.shape[1] // 2
    n_f = d_ff // tf
    return pl.pallas_call(
        _ffn_up_kernel,
        grid=(t // tm, n_f),
        in_specs=[
            pl.BlockSpec((tm, d), lambda i, j: (i, 0)),
            pl.BlockSpec((d, tf), lambda i, j: (0, j)),
            pl.BlockSpec((d, tf), lambda i, j: (0, n_f + j)),
        ],
        out_specs=pl.BlockSpec((tm, tf), lambda i, j: (i, j)),
        out_shape=jax.ShapeDtypeStruct((t, d_ff), BF16),
        compiler_params=_cparams(("parallel", "arbitrary"), 40),
        name="ffn_up",
    )(h2, w_ffn_in, w_ffn_in)


def _ffn_down_kernel(a_ref, wo_ref, o_ref):
    o_ref[...] = jnp.dot(a_ref[...], wo_ref[...], preferred_element_type=F32)


def _ffn_down(act, w_out, *, tm=1024, tn=512):
    t, d_ff = act.shape
    d = w_out.shape[1]
    return pl.pallas_call(
        _ffn_down_kernel,
        grid=(t // tm, d // tn),
        in_specs=[
            pl.BlockSpec((tm, d_ff), lambda i, j: (i, 0)),
            pl.BlockSpec((d_ff, tn), lambda i, j: (0, j)),
        ],
        out_specs=pl.BlockSpec((tm, tn), lambda i, j: (i, j)),
        out_shape=jax.ShapeDtypeStruct((t, d), F32),
        compiler_params=_cparams(("parallel", "arbitrary"), 48),
        name="ffn_down",
    )(act, w_out)


def _ple_kernel(f_ref, x1_ref, p_ref, g_ref, wg_ref, wp_ref, o_ref):
    for r0 in range(0, f_ref.shape[0], ROW_SUB):
        rows = slice(r0, r0 + ROW_SUB)
        x2 = x1_ref[rows, :] + _rms(f_ref[rows, :], g_ref[...])
        gate = jnp.dot(x2.astype(BF16), wg_ref[...], preferred_element_type=F32)
        ple = jnp.dot(p_ref[rows, :].astype(BF16), wp_ref[...], preferred_element_type=F32)
        o_ref[rows, :] = x2 + jax.nn.sigmoid(gate) * ple


def _ple(f, x1, p2d, g_post, w_gate, w_proj, *, tm=512):
    t, d = x1.shape
    k = p2d.shape[1]
    return pl.pallas_call(
        _ple_kernel,
        grid=(t // tm,),
        in_specs=[
            pl.BlockSpec((tm, d), lambda i: (i, 0)),
            pl.BlockSpec((tm, d), lambda i: (i, 0)),
            pl.BlockSpec((tm, k), lambda i: (i, 0)),
            pl.BlockSpec((1, d), lambda i: (0, 0)),
            pl.BlockSpec((d, d), lambda i: (0, 0)),
            pl.BlockSpec((k, d), lambda i: (0, 0)),
        ],
        out_specs=pl.BlockSpec((tm, d), lambda i: (i, 0)),
        out_shape=jax.ShapeDtypeStruct((t, d), F32),
        compiler_params=_cparams(("parallel",), 52),
        name="ple",
    )(f, x1, p2d, g_post, w_gate, w_proj)


def kernel(x, p, w_in, conv_w, conv_b, dt_bias, a_log, d_skip, ssm_norm_g, w_ssm_out,
           w_pool_grp, pool_scale, w_pool_out, w_mix_out, g_pre_mix, g_post_mix,
           g_pre_ffn, g_post_ffn, w_ffn_in, w_ffn_out, w_ple_proj, w_ple_gate):
    batch, seq, d = x.shape
    depth = w_in.shape[0]
    d_inner = w_ssm_out.shape[1]
    d_pool = w_pool_out.shape[1]
    heads = dt_bias.shape[1]
    groups = SSM_GROUPS
    hpg = heads // groups
    gw = d_inner // groups
    n_bc = groups * D_STATE
    conv_dim = d_inner + 2 * n_bc
    t = batch * seq
    assert heads * SSM_HEAD_DIM == d_inner and conv_w.shape[2] == conv_dim
    assert w_in.shape[2] == d_inner + conv_dim + heads + d_pool + 2 * d

    dt_col0 = d_inner + conv_dim
    pool_col0 = dt_col0
    gs_col0 = pool_col0 + d_pool
    gp_col0 = gs_col0 + d

    x2d = x.reshape(t, d)
    for li in range(depth):
        wi = w_in[li]
        w_main = jnp.concatenate([0.5 * wi[:, :d_inner], wi[:, d_inner:dt_col0],
                                  wi[:, dt_col0 + heads:]], axis=1).astype(BF16)
        w_dt = wi[:, dt_col0:dt_col0 + heads].astype(BF16)

        def per_group_cols(v):
            xs = v[..., :d_inner].reshape(v.shape[:-1] + (groups, gw))
            bm = v[..., d_inner:d_inner + n_bc].reshape(v.shape[:-1] + (groups, D_STATE))
            cm = v[..., d_inner + n_bc:].reshape(v.shape[:-1] + (groups, D_STATE))
            cat = jnp.concatenate([xs, bm, cm], axis=-1)
            return jnp.moveaxis(cat, -2, 0)
        cw = per_group_cols(0.5 * conv_w[li])
        cb = per_group_cols(0.5 * conv_b[li][None, :])
        dtb = dt_bias[li].reshape(groups, hpg, 1)
        alog = a_log[li].reshape(groups, hpg, 1)
        dsk = jnp.repeat(d_skip[li], SSM_HEAD_DIM).reshape(groups, 1, gw)
        ng = ssm_norm_g[li].reshape(groups, 1, gw)

        proj, dt_t = _in_proj(x2d, g_pre_mix[li][None, :], w_main, w_dt)
        y_ssm = _ssd(proj, dt_t, cw, cb, dtb, alog, dsk, ng,
                     batch=batch, seq=seq, d_inner=d_inner)
        y_pool = _pool(proj, w_pool_grp[li].astype(BF16), pool_scale[li][None, :],
                       seq=seq, d_pool=d_pool, col0=pool_col0)
        merged = _merge(y_ssm, y_pool, proj, w_ssm_out[li].astype(BF16),
                        w_pool_out[li].astype(BF16), gs_col0=gs_col0, gp_col0=gp_col0)
        x1, h2 = _mix(merged, x2d, w_mix_out[li].astype(BF16),
                      g_post_mix[li][None, :], g_pre_ffn[li][None, :])
        act = _ffn_up(h2, w_ffn_in[li].astype(BF16))
        f = _ffn_down(act, w_ffn_out[li].astype(BF16))
        x2d = _ple(f, x1, p[li].reshape(t, -1), g_post_ffn[li][None, :],
                   w_ple_gate[li].astype(BF16), w_ple_proj[li].astype(BF16))
    return x2d.reshape(batch, seq, d)
```

```python
import functools

import jax
import jax.numpy as jnp
import numpy as np
from jax import lax
from jax.experimental import pallas as pl
from jax.experimental.pallas import tpu as pltpu

F32 = jnp.float32
BF16 = jnp.bfloat16

EPS = 1e-6
SSM_HEAD_DIM = 64
SSM_GROUPS = 8
D_STATE = 128
CONV_WIDTH = 4
POOL_WINDOWS = (2, 4, 8, 16)
POOL_HALO = 16
SSD_Q = 128
LANES = 128
CONV_PHASES = 4
ROW_SUB = 256

LOG2_E = 1.4426950408889634

MIB = 1024 * 1024


def _cparams(semantics, vmem_mib):
    return pltpu.CompilerParams(dimension_semantics=semantics,
                                vmem_limit_bytes=vmem_mib * MIB)


def _rms(v, g):
    ms = jnp.mean(v * v, axis=-1, keepdims=True)
    return v * lax.rsqrt(ms + EPS) * g


def _silu(v):
    return v * jax.nn.sigmoid(v)


def _silu_of_twice(h):
    return h + h * jnp.tanh(h)


def _softplus(v):
    return jnp.maximum(v, 0.0) + jnp.log1p(jnp.exp(-jnp.abs(v)))


def _split_terms(v, n_terms):
    pieces = []
    rem = v
    for _ in range(n_terms - 1):
        hi = rem.astype(BF16).astype(F32)
        pieces.append(hi)
        rem = rem - hi
    pieces.append(rem)
    return pieces


def _in_proj_kernel(x_ref, g_ref, w_ref, wdt_ref, o_ref, dt_ref, h_scr, *, n_chunks):
    @pl.when(pl.program_id(1) == 0)
    def _():
        def body(q, carry):
            rows = pl.ds(pl.multiple_of(q * SSD_Q, SSD_Q), SSD_Q)
            h = _rms(x_ref[rows, :], g_ref[...]).astype(BF16)
            h_scr[rows, :] = h
            dt_ref[q] = lax.dot_general(wdt_ref[...], h, (((1,), (1,)), ((), ())),
                                        preferred_element_type=F32)
            return carry
        lax.fori_loop(0, n_chunks, body, 0, unroll=2)

    o_ref[...] = lax.dot_general(h_scr[...], w_ref[...], (((1,), (1,)), ((), ())),
                                 preferred_element_type=F32).astype(o_ref.dtype)


def _in_proj(x2d, g, w_main_t, w_dt_t, *, tm=1024, tn=1024):
    t, d = x2d.shape
    n = w_main_t.shape[0]
    heads = w_dt_t.shape[0]
    n_chunks = tm // SSD_Q
    return pl.pallas_call(
        functools.partial(_in_proj_kernel, n_chunks=n_chunks),
        grid=(t // tm, n // tn),
        in_specs=[
            pl.BlockSpec((tm, d), lambda i, j: (i, 0)),
            pl.BlockSpec((1, d), lambda i, j: (0, 0)),
            pl.BlockSpec((tn, d), lambda i, j: (j, 0)),
            pl.BlockSpec((heads, d), lambda i, j: (0, 0)),
        ],
        out_specs=[
            pl.BlockSpec((tm, tn), lambda i, j: (i, j)),
            pl.BlockSpec((n_chunks, heads, SSD_Q), lambda i, j: (i, 0, 0)),
        ],
        out_shape=[
            jax.ShapeDtypeStruct((t, n), BF16),
            jax.ShapeDtypeStruct((t // SSD_Q, heads, SSD_Q), F32),
        ],
        scratch_shapes=[pltpu.VMEM((tm, d), BF16)],
        compiler_params=_cparams(("parallel", "arbitrary"), 48),
        name="in_proj",
    )(x2d, g, w_main_t, w_dt_t)


def _ssd_kernel(z_ref, xs_ref, b_ref, c_ref, dt_ref, cw_ref, cb_ref, dtb_ref, alog_ref,
                dsk_ref, ng_ref, u3_ref, ex_ref, rep_ref, y_ref,
                state_scr, ubuf, xc, lhs_scr, cumj_scr, dl_scr, ci_scr, g_scr, s_scr,
                *, lc, gw):
    q_len = SSD_Q
    n_chunks = lc // q_len
    hpg = gw // SSM_HEAD_DIM
    ns_x = gw // LANES
    ns = ns_x + 2
    halo = 8
    pr = q_len // CONV_PHASES

    @pl.when(pl.program_id(2) == 0)
    def _():
        state_scr[...] = jnp.zeros_like(state_scr)
        ubuf[:, 0:halo, :] = jnp.zeros((ns, halo, LANES), F32)

    for s in range(ns_x):
        ubuf[s, halo:halo + lc, :] = xs_ref[:, s * LANES:(s + 1) * LANES].astype(F32)
    ubuf[ns_x, halo:halo + lc, :] = b_ref[...].astype(F32)
    ubuf[ns_x + 1, halo:halo + lc, :] = c_ref[...].astype(F32)

    dt_all = _softplus(dt_ref[...] + dtb_ref[...][None]).reshape(n_chunks * hpg, q_len)
    a_all = jnp.broadcast_to(-jnp.exp(alog_ref[...])[None],
                             (n_chunks, hpg, 1)).reshape(n_chunks * hpg, 1)
    da3 = jnp.concatenate(_split_terms(dt_all * a_all, 3), axis=1).astype(BF16)
    cum2 = jnp.dot(da3, u3_ref[...], preferred_element_type=F32) * LOG2_E
    cum2_last = cum2[:, q_len - 1:q_len]
    te_all = jnp.exp2(cum2_last - cum2) * dt_all
    cumj_scr[...] = (cum2 - jnp.log2(dt_all)).reshape(n_chunks, hpg, q_len)
    pieces = [p.reshape(n_chunks, hpg, q_len)
              for p in _split_terms(cum2, 3) + _split_terms(te_all, 2)]
    pieces.append(jnp.zeros((n_chunks, hpg, q_len), F32))
    lhs_scr[...] = jnp.concatenate(pieces, axis=1).astype(BF16)
    dl_scr[...] = jnp.sum(jnp.exp2(cum2_last).reshape(n_chunks, hpg, 1) * rep_ref[...][None],
                          axis=1)

    def conv(k):
        r0 = k * q_len if isinstance(k, int) else pl.multiple_of(k * q_len, q_len)
        for s in range(ns):
            lanes = slice(s * LANES, (s + 1) * LANES)
            taps = {d: ubuf.at[s][pl.ds(r0 + halo + d, pr, stride=CONV_PHASES), :]
                    for d in range(1 - CONV_WIDTH, CONV_PHASES)}
            for ph in range(CONV_PHASES):
                acc = cb_ref[:, lanes]
                for kk in range(CONV_WIDTH):
                    acc = acc + cw_ref[kk:kk + 1, lanes] * taps[ph - (CONV_WIDTH - 1) + kk]
                xc.at[s][pl.ds(r0 + ph, pr, stride=CONV_PHASES), :] = _silu_of_twice(acc)

    dsk = dsk_ref[...]
    ng = ng_ref[...]
    row_i = lax.broadcasted_iota(jnp.int32, (q_len, q_len), 0)
    col_j = lax.broadcasted_iota(jnp.int32, (q_len, q_len), 1)
    causal = row_i >= col_j
    low_half = lax.broadcasted_iota(jnp.int32, (1, LANES), 1) < SSM_HEAD_DIM

    def chunk_rows(k):
        start = k * q_len
        return pl.ds(start if isinstance(k, int) else pl.multiple_of(start, q_len), q_len)

    def load_x(rows):
        return jnp.concatenate([xc[s, rows, :] for s in range(ns_x)], axis=1)

    def prep(k, slot):
        rows = chunk_rows(k)
        xq = load_x(rows)
        cq = xc[ns_x + 1, rows, :]
        bq16 = xc[ns_x, rows, :].astype(BF16)
        cq16 = cq.astype(BF16)

        xp = lax.dot_general(lhs_scr[k], ex_ref[...], (((0,), (0,)), ((), ())),
                             preferred_element_type=F32)
        ci_scr[slot] = xp[:, 0:hpg * q_len]
        te_exp = xp[:, hpg * q_len:hpg * q_len + gw]

        cb = lax.dot_general(cq16, bq16, (((1,), (1,)), ((), ())),
                             preferred_element_type=F32)
        g_scr[slot] = jnp.concatenate([cb, cq], axis=1)

        xte = (xq * te_exp).astype(BF16)
        s_scr[slot] = lax.dot_general(bq16, xte, (((0,), (0,)), ((), ())),
                                      preferred_element_type=F32)

    def heads(k, slot):
        rows = chunk_rows(k)
        xq = load_x(rows)
        g_mat = g_scr[slot]
        cum_j = cumj_scr[k]
        st = state_scr[...]
        rhs = jnp.concatenate([xq.astype(BF16), st.astype(BF16)], axis=0)
        ys = []
        for m in range(hpg // 2):
            rp = rhs[:, m * LANES:(m + 1) * LANES]
            parts = []
            for r in (2 * m, 2 * m + 1):
                ci = ci_scr[slot, :, r * q_len:(r + 1) * q_len]
                seg = jnp.where(causal, ci - cum_j[r:r + 1, :], -jnp.inf)
                lhs = (g_mat * jnp.exp2(jnp.concatenate([seg, ci], axis=1))).astype(BF16)
                parts.append(jnp.dot(lhs, rp, preferred_element_type=F32))
            ys.append(jnp.where(low_half, parts[0], parts[1]))
        y_mm = jnp.concatenate(ys, axis=1)

        state_scr[...] = st * dl_scr[pl.ds(k, 1), :] + s_scr[slot]

        z_half = z_ref[rows, :].astype(F32)
        y = (y_mm + dsk * xq) * _silu_of_twice(z_half)
        y_ref[rows, :] = _rms(y, ng).astype(y_ref.dtype)

    conv(0)
    conv(1)
    prep(0, 0)

    def pair(kk, carry):
        k0 = 2 * kk
        heads(k0, 0)
        prep(k0 + 1, 1)
        conv(k0 + 2)
        heads(k0 + 1, 1)
        prep(k0 + 2, 0)
        conv(k0 + 3)
        return carry

    lax.fori_loop(0, n_chunks // 2 - 1, pair, 0)
    heads(n_chunks - 2, 0)
    prep(n_chunks - 1, 1)
    heads(n_chunks - 1, 1)
    ubuf[:, 0:halo, :] = ubuf[:, lc:lc + halo, :]


def _ssd(proj, dt_t, cw, cb, dtb, alog, dsk, ng, *, batch, seq, d_inner, lc=2048):
    t = proj.shape[0]
    groups = SSM_GROUPS
    gw = d_inner // groups
    hpg = gw // SSM_HEAD_DIM
    q_len = SSD_Q
    n_tiles = seq // lc
    n_chunks = lc // q_len
    cwid = gw + 2 * D_STATE
    ns = cwid // LANES
    assert D_STATE == q_len == LANES and hpg % 2 == 0 and gw % LANES == 0
    xs_blk0 = d_inner // gw
    b_blk0 = 2 * d_inner // D_STATE
    c_blk0 = b_blk0 + groups

    tri = np.triu(np.ones((q_len, q_len), np.float32))
    u3 = jnp.asarray(np.concatenate([tri, tri, tri], axis=0), BF16)
    rep = np.kron(np.eye(hpg, dtype=np.float32), np.ones((1, SSM_HEAD_DIM), np.float32))
    rq = np.kron(np.eye(hpg, dtype=np.float32), np.ones((1, q_len), np.float32))
    zq = np.zeros_like(rq)
    zr = np.zeros_like(rep)
    ex = np.concatenate([
        np.concatenate([rq, zr], axis=1), np.concatenate([rq, zr], axis=1),
        np.concatenate([rq, zr], axis=1),
        np.concatenate([zq, rep], axis=1), np.concatenate([zq, rep], axis=1),
        np.concatenate([zq, zr], axis=1),
    ], axis=0)
    ex = jnp.asarray(ex, BF16)
    rep = jnp.asarray(rep, F32)

    row = lambda b, g, c: b * n_tiles + c
    const2 = lambda b, g, c: (0, 0)
    per_group = lambda b, g, c: (g, 0, 0)
    return pl.pallas_call(
        functools.partial(_ssd_kernel, lc=lc, gw=gw),
        grid=(batch, groups, n_tiles),
        in_specs=[
            pl.BlockSpec((lc, gw), lambda b, g, c: (row(b, g, c), g)),
            pl.BlockSpec((lc, gw), lambda b, g, c: (row(b, g, c), xs_blk0 + g)),
            pl.BlockSpec((lc, D_STATE), lambda b, g, c: (row(b, g, c), b_blk0 + g)),
            pl.BlockSpec((lc, D_STATE), lambda b, g, c: (row(b, g, c), c_blk0 + g)),
            pl.BlockSpec((lc // q_len, hpg, q_len), lambda b, g, c: (row(b, g, c), g, 0)),
            pl.BlockSpec((None, CONV_WIDTH, cwid), per_group),
            pl.BlockSpec((None, 1, cwid), per_group),
            pl.BlockSpec((None, hpg, 1), per_group),
            pl.BlockSpec((None, hpg, 1), per_group),
            pl.BlockSpec((None, 1, gw), per_group),
            pl.BlockSpec((None, 1, gw), per_group),
            pl.BlockSpec(u3.shape, const2),
            pl.BlockSpec(ex.shape, const2),
            pl.BlockSpec(rep.shape, const2),
        ],
        out_specs=pl.BlockSpec((lc, gw), lambda b, g, c: (row(b, g, c), g)),
        out_shape=jax.ShapeDtypeStruct((t, d_inner), BF16),
        scratch_shapes=[
            pltpu.VMEM((D_STATE, gw), F32),
            pltpu.VMEM((ns, lc + 8, LANES), F32),
            pltpu.VMEM((ns, lc, LANES), F32),
            pltpu.VMEM((n_chunks, 6 * hpg, q_len), BF16),
            pltpu.VMEM((n_chunks, hpg, q_len), F32),
            pltpu.VMEM((n_chunks, gw), F32),
            pltpu.VMEM((2, q_len, hpg * q_len), F32),
            pltpu.VMEM((2, q_len, q_len + D_STATE), F32),
            pltpu.VMEM((2, D_STATE, gw), F32),
        ],
        compiler_params=_cparams(("parallel", "parallel", "arbitrary"), 44),
        name="ssd",
    )(proj, proj, proj, proj, dt_t, cw, cb, dtb, alog, dsk, ng, u3, ex, rep)


def _pool_kernel(u_ref, halo_ref, wg_ref, sc_ref, o_ref, buf, *, tp, tiles_per_seq, gdim):
    i = pl.program_id(0)
    seq_tile = i % tiles_per_seq
    keep = (seq_tile != 0).astype(F32)
    buf[0:POOL_HALO, :] = halo_ref[...].astype(F32) * keep
    buf[POOL_HALO:POOL_HALO + tp, :] = u_ref[...].astype(F32)

    rc = 128
    for r0 in range(0, tp, rc):
        pos = (seq_tile * tp + r0 + 1
               + lax.broadcasted_iota(jnp.int32, (rc, 1), 0)).astype(F32)
        for gi, win in enumerate(POOL_WINDOWS):
            cols = slice(gi * gdim, (gi + 1) * gdim)
            base = POOL_HALO + r0
            cur = buf[base:base + rc, cols]
            s = cur
            for k in range(1, win):
                s = s + buf[base - k:base - k + rc, cols]
            pooled = (s / jnp.minimum(pos, float(win)) - cur).astype(BF16)
            mixed = jnp.dot(pooled, wg_ref[gi], preferred_element_type=F32)
            o_ref[r0:r0 + rc, cols] = (mixed * sc_ref[:, cols]).astype(o_ref.dtype)


def _pool(proj, w_grp, scale, *, seq, d_pool, col0, tp=512):
    t = proj.shape[0]
    n_groups = len(POOL_WINDOWS)
    gdim = d_pool // n_groups
    tiles_per_seq = seq // tp
    cblk = col0 // d_pool
    hb = tp // POOL_HALO
    return pl.pallas_call(
        functools.partial(_pool_kernel, tp=tp, tiles_per_seq=tiles_per_seq, gdim=gdim),
        grid=(t // tp,),
        in_specs=[
            pl.BlockSpec((tp, d_pool), lambda i: (i, cblk)),
            pl.BlockSpec((POOL_HALO, d_pool), lambda i: (jnp.maximum(i * hb - 1, 0), cblk)),
            pl.BlockSpec((n_groups, gdim, gdim), lambda i: (0, 0, 0)),
            pl.BlockSpec((1, d_pool), lambda i: (0, 0)),
        ],
        out_specs=pl.BlockSpec((tp, d_pool), lambda i: (i, 0)),
        out_shape=jax.ShapeDtypeStruct((t, d_pool), BF16),
        scratch_shapes=[pltpu.VMEM((tp + POOL_HALO, d_pool), F32)],
        compiler_params=_cparams(("parallel",), 32),
        name="pool",
    )(proj, proj, w_grp, scale)


def _merge_kernel(ys_ref, yp_ref, gs_ref, gp_ref, ws_ref, wp_ref, o_ref):
    a = jnp.dot(ys_ref[...], ws_ref[...], preferred_element_type=F32)
    b = jnp.dot(yp_ref[...], wp_ref[...], preferred_element_type=F32)
    merged = (jax.nn.sigmoid(gs_ref[...].astype(F32)) * a
              + jax.nn.sigmoid(gp_ref[...].astype(F32)) * b)
    o_ref[...] = merged.astype(o_ref.dtype)


def _merge(y_ssm, y_pool, proj, w_so, w_po, *, gs_col0, gp_col0, tm=1024, tn=512):
    t, d_inner = y_ssm.shape
    d_pool = y_pool.shape[1]
    d = w_so.shape[1]
    gs_blk = gs_col0 // tn
    gp_blk = gp_col0 // tn
    return pl.pallas_call(
        _merge_kernel,
        grid=(t // tm, d // tn),
        in_specs=[
            pl.BlockSpec((tm, d_inner), lambda i, j: (i, 0)),
            pl.BlockSpec((tm, d_pool), lambda i, j: (i, 0)),
            pl.BlockSpec((tm, tn), lambda i, j: (i, gs_blk + j)),
            pl.BlockSpec((tm, tn), lambda i, j: (i, gp_blk + j)),
            pl.BlockSpec((d_inner, tn), lambda i, j: (0, j)),
            pl.BlockSpec((d_pool, tn), lambda i, j: (0, j)),
        ],
        out_specs=pl.BlockSpec((tm, tn), lambda i, j: (i, j)),
        out_shape=jax.ShapeDtypeStruct((t, d), BF16),
        compiler_params=_cparams(("parallel", "arbitrary"), 48),
        name="merge",
    )(y_ssm, y_pool, proj, proj, w_so, w_po)


def _mix_kernel(m_ref, x_ref, w_ref, gpost_ref, gpre_ref, x1_ref, h2_ref):
    for r0 in range(0, m_ref.shape[0], ROW_SUB):
        rows = slice(r0, r0 + ROW_SUB)
        mix = jnp.dot(m_ref[rows, :], w_ref[...], preferred_element_type=F32)
        x1 = x_ref[rows, :] + _rms(mix, gpost_ref[...])
        x1_ref[rows, :] = x1
        h2_ref[rows, :] = _rms(x1, gpre_ref[...]).astype(h2_ref.dtype)


def _mix(merged, x2d, w_mix, g_post, g_pre, *, tm=512):
    t, d = x2d.shape
    return pl.pallas_call(
        _mix_kernel,
        grid=(t // tm,),
        in_specs=[
            pl.BlockSpec((tm, d), lambda i: (i, 0)),
            pl.BlockSpec((tm, d), lambda i: (i, 0)),
            pl.BlockSpec((d, d), lambda i: (0, 0)),
            pl.BlockSpec((1, d), lambda i: (0, 0)),
            pl.BlockSpec((1, d), lambda i: (0, 0)),
        ],
        out_specs=[
            pl.BlockSpec((tm, d), lambda i: (i, 0)),
            pl.BlockSpec((tm, d), lambda i: (i, 0)),
        ],
        out_shape=[
            jax.ShapeDtypeStruct((t, d), F32),
            jax.ShapeDtypeStruct((t, d), BF16),
        ],
        compiler_params=_cparams(("parallel",), 48),
        name="mix",
    )(merged, x2d, w_mix, g_post, g_pre)


def _ffn_up_kernel(h_ref, wg_ref, wu_ref, o_ref):
    h = h_ref[...]
    gate = jnp.dot(h, wg_ref[...], preferred_element_type=F32)
    up = jnp.dot(h, wu_ref[...], preferred_element_type=F32)
    o_ref[...] = (_silu(gate) * up).astype(o_ref.dtype)


def _ffn_up(h2, w_ffn_in, *, tm=1024, tf=512):
    t, d = h2.shape
    d_ff = w_ffn_in.shape[1] // 2
    n_f = d_ff // tf
    return pl.pallas_call(
        _ffn_up_kernel,
        grid=(t // tm, n_f),
        in_specs=[
            pl.BlockSpec((tm, d), lambda i, j: (i, 0)),
            pl.BlockSpec((d, tf), lambda i, j: (0, j)),
            pl.BlockSpec((d, tf), lambda i, j: (0, n_f + j)),
        ],
        out_specs=pl.BlockSpec((tm, tf), lambda i, j: (i, j)),
        out_shape=jax.ShapeDtypeStruct((t, d_ff), BF16),
        compiler_params=_cparams(("parallel", "arbitrary"), 40),
        name="ffn_up",
    )(h2, w_ffn_in, w_ffn_in)


def _ffn_down_kernel(a_ref, wo_ref, o_ref):
    o_ref[...] = jnp.dot(a_ref[...], wo_ref[...], preferred_element_type=F32)


def _ffn_down(act, w_out, *, tm=1024, tn=512):
    t, d_ff = act.shape
    d = w_out.shape[1]
    return pl.pallas_call(
        _ffn_down_kernel,
        grid=(t // tm, d // tn),
        in_specs=[
            pl.BlockSpec((tm, d_ff), lambda i, j: (i, 0)),
            pl.BlockSpec((d_ff, tn), lambda i, j: (0, j)),
        ],
        out_specs=pl.BlockSpec((tm, tn), lambda i, j: (i, j)),
        out_shape=jax.ShapeDtypeStruct((t, d), F32),
        compiler_params=_cparams(("parallel", "arbitrary"), 48),
        name="ffn_down",
    )(act, w_out)


def _ple_kernel(f_ref, x1_ref, p_ref, g_ref, wg_ref, wp_ref, o_ref):
    for r0 in range(0, f_ref.shape[0], ROW_SUB):
        rows = slice(r0, r0 + ROW_SUB)
        x2 = x1_ref[rows, :] + _rms(f_ref[rows, :], g_ref[...])
        gate = jnp.dot(x2.astype(BF16), wg_ref[...], preferred_element_type=F32)
        ple = jnp.dot(p_ref[rows, :].astype(BF16), wp_ref[...], preferred_element_type=F32)
        o_ref[rows, :] = x2 + jax.nn.sigmoid(gate) * ple


def _ple(f, x1, p2d, g_post, w_gate, w_proj, *, tm=512):
    t, d = x1.shape
    k = p2d.shape[1]
    return pl.pallas_call(
        _ple_kernel,
        grid=(t // tm,),
        in_specs=[
            pl.BlockSpec((tm, d), lambda i: (i, 0)),
            pl.BlockSpec((tm, d), lambda i: (i, 0)),
            pl.BlockSpec((tm, k), lambda i: (i, 0)),
            pl.BlockSpec((1, d), lambda i: (0, 0)),
            pl.BlockSpec((d, d), lambda i: (0, 0)),
            pl.BlockSpec((k, d), lambda i: (0, 0)),
        ],
        out_specs=pl.BlockSpec((tm, d), lambda i: (i, 0)),
        out_shape=jax.ShapeDtypeStruct((t, d), F32),
        compiler_params=_cparams(("parallel",), 52),
        name="ple",
    )(f, x1, p2d, g_post, w_gate, w_proj)


def kernel(x, p, w_in, conv_w, conv_b, dt_bias, a_log, d_skip, ssm_norm_g, w_ssm_out,
           w_pool_grp, pool_scale, w_pool_out, w_mix_out, g_pre_mix, g_post_mix,
           g_pre_ffn, g_post_ffn, w_ffn_in, w_ffn_out, w_ple_proj, w_ple_gate):
    batch, seq, d = x.shape
    depth = w_in.shape[0]
    d_inner = w_ssm_out.shape[1]
    d_pool = w_pool_out.shape[1]
    heads = dt_bias.shape[1]
    groups = SSM_GROUPS
    hpg = heads // groups
    gw = d_inner // groups
    n_bc = groups * D_STATE
    conv_dim = d_inner + 2 * n_bc
    t = batch * seq
    assert heads * SSM_HEAD_DIM == d_inner and conv_w.shape[2] == conv_dim
    assert w_in.shape[2] == d_inner + conv_dim + heads + d_pool + 2 * d

    dt_col0 = d_inner + conv_dim
    pool_col0 = dt_col0
    gs_col0 = pool_col0 + d_pool
    gp_col0 = gs_col0 + d

    x2d = x.reshape(t, d)
    for li in range(depth):
        wi = w_in[li]
        wt = wi.T
        w_main_t = jnp.concatenate([0.5 * wt[:d_inner], wt[d_inner:dt_col0],
                                    wt[dt_col0 + heads:]], axis=0).astype(BF16)
        w_dt_t = wt[dt_col0:dt_col0 + heads].astype(BF16)

        def per_group_cols(v):
            xs = v[..., :d_inner].reshape(v.shape[:-1] + (groups, gw))
            bm = v[..., d_inner:d_inner + n_bc].reshape(v.shape[:-1] + (groups, D_STATE))
            cm = v[..., d_inner + n_bc:].reshape(v.shape[:-1] + (groups, D_STATE))
            cat = jnp.concatenate([xs, bm, cm], axis=-1)
            return jnp.moveaxis(cat, -2, 0)
        cw = per_group_cols(0.5 * conv_w[li])
        cb = per_group_cols(0.5 * conv_b[li][None, :])
        dtb = dt_bias[li].reshape(groups, hpg, 1)
        alog = a_log[li].reshape(groups, hpg, 1)
        dsk = jnp.repeat(d_skip[li], SSM_HEAD_DIM).reshape(groups, 1, gw)
        ng = ssm_norm_g[li].reshape(groups, 1, gw)

        proj, dt_t = _in_proj(x2d, g_pre_mix[li][None, :], w_main_t, w_dt_t)
        y_ssm = _ssd(proj, dt_t, cw, cb, dtb, alog, dsk, ng,
                     batch=batch, seq=seq, d_inner=d_inner)
        y_pool = _pool(proj, w_pool_grp[li].astype(BF16), pool_scale[li][None, :],
                       seq=seq, d_pool=d_pool, col0=pool_col0)
        merged = _merge(y_ssm, y_pool, proj, w_ssm_out[li].astype(BF16),
                        w_pool_out[li].astype(BF16), gs_col0=gs_col0, gp_col0=gp_col0)
        x1, h2 = _mix(merged, x2d, w_mix_out[li].astype(BF16),
                      g_post_mix[li][None, :], g_pre_ffn[li][None, :])
        act = _ffn_up(h2, w_ffn_in[li].astype(BF16))
        f = _ffn_down(act, w_ffn_out[li].astype(BF16))
        x2d = _ple(f, x1, p[li].reshape(t, -1), g_post_ffn[li][None, :],
                   w_ple_gate[li].astype(BF16), w_ple_proj[li].astype(BF16))
    return x2d.reshape(batch, seq, d)
```

```python
import functools

import jax
import jax.numpy as jnp
import numpy as np
from jax import lax
from jax.experimental import pallas as pl
from jax.experimental.pallas import tpu as pltpu

F32 = jnp.float32
BF16 = jnp.bfloat16

EPS = 1e-6
SSM_HEAD_DIM = 64
SSM_GROUPS = 8
D_STATE = 128
CONV_WIDTH = 4
POOL_WINDOWS = (2, 4, 8, 16)
POOL_HALO = 16
SSD_Q = 128
LANES = 128
CONV_PHASES = 4
ROW_SUB = 256

LOG2_E = 1.4426950408889634

MIB = 1024 * 1024


def _cparams(semantics, vmem_mib):
    return pltpu.CompilerParams(dimension_semantics=semantics,
                                vmem_limit_bytes=vmem_mib * MIB)


def _rms(v, g):
    ms = jnp.mean(v * v, axis=-1, keepdims=True)
    return v * lax.rsqrt(ms + EPS) * g


def _silu(v):
    return v * jax.nn.sigmoid(v)


def _silu_of_twice(h):
    return h + h * jnp.tanh(h)


def _softplus(v):
    return jnp.maximum(v, 0.0) + jnp.log1p(jnp.exp(-jnp.abs(v)))


def _split_terms(v, n_terms):
    pieces = []
    rem = v
    for _ in range(n_terms - 1):
        hi = rem.astype(BF16).astype(F32)
        pieces.append(hi)
        rem = rem - hi
    pieces.append(rem)
    return pieces


def _in_proj_kernel(x_ref, g_ref, w_ref, wdt_ref, o_ref, dt_ref, h_scr, *, n_chunks, n_half):
    @pl.when(pl.program_id(1) == 0)
    def _():
        def body(q, carry):
            rows = pl.ds(pl.multiple_of(q * SSD_Q, SSD_Q), SSD_Q)
            h = _rms(x_ref[rows, :], g_ref[...]).astype(BF16)
            h_scr[rows, :] = h
            dt_ref[q] = lax.dot_general(wdt_ref[...], h, (((1,), (1,)), ((), ())),
                                        preferred_element_type=F32)
            return carry
        lax.fori_loop(0, n_chunks, body, 0, unroll=2)

    acc = lax.dot_general(h_scr[...], w_ref[...], (((1,), (1,)), ((), ())),
                          preferred_element_type=F32)
    scale = jnp.where(pl.program_id(1) < n_half, 0.5, 1.0).astype(F32)
    o_ref[...] = (acc * scale).astype(o_ref.dtype)


def _in_proj(x2d, g, w_t, *, dt_row0, n_half_cols, tm=1024, tn=1024):
    t, d = x2d.shape
    heads = n_half_cols // SSM_HEAD_DIM
    n = w_t.shape[0] - heads
    n_chunks = tm // SSD_Q
    n_before = dt_row0 // tn
    n_half = n_half_cols // tn
    assert dt_row0 % tn == 0 and n_half_cols % tn == 0 and dt_row0 % heads == 0
    return pl.pallas_call(
        functools.partial(_in_proj_kernel, n_chunks=n_chunks, n_half=n_half),
        grid=(t // tm, n // tn),
        in_specs=[
            pl.BlockSpec((tm, d), lambda i, j: (i, 0)),
            pl.BlockSpec((1, d), lambda i, j: (0, 0)),
            pl.BlockSpec((pl.Element(tn), pl.Element(d)),
                         lambda i, j: ((j * (tn // heads) + (j >= n_before).astype(jnp.int32))
                                       * heads, 0)),
            pl.BlockSpec((heads, d), lambda i, j: (dt_row0 // heads, 0)),
        ],
        out_specs=[
            pl.BlockSpec((tm, tn), lambda i, j: (i, j)),
            pl.BlockSpec((n_chunks, heads, SSD_Q), lambda i, j: (i, 0, 0)),
        ],
        out_shape=[
            jax.ShapeDtypeStruct((t, n), BF16),
            jax.ShapeDtypeStruct((t // SSD_Q, heads, SSD_Q), F32),
        ],
        scratch_shapes=[pltpu.VMEM((tm, d), BF16)],
        compiler_params=_cparams(("parallel", "arbitrary"), 48),
        name="in_proj",
    )(x2d, g, w_t, w_t)


def _ssd_kernel(z_ref, xs_ref, b_ref, c_ref, dt_ref, cw_ref, cb_ref, dtb_ref, alog_ref,
                dsk_ref, ng_ref, u3_ref, ex_ref, rep_ref, y_ref,
                state_scr, ubuf, xc, lhs_scr, cumj_scr, dl_scr, ci_scr, g_scr, s_scr,
                *, lc, gw):
    q_len = SSD_Q
    n_chunks = lc // q_len
    hpg = gw // SSM_HEAD_DIM
    ns_x = gw // LANES
    ns = ns_x + 2
    halo = 8
    pr = q_len // CONV_PHASES

    @pl.when(pl.program_id(2) == 0)
    def _():
        state_scr[...] = jnp.zeros_like(state_scr)
        ubuf[:, 0:halo, :] = jnp.zeros((ns, halo, LANES), F32)

    for s in range(ns_x):
        ubuf[s, halo:halo + lc, :] = xs_ref[:, s * LANES:(s + 1) * LANES].astype(F32)
    ubuf[ns_x, halo:halo + lc, :] = b_ref[...].astype(F32)
    ubuf[ns_x + 1, halo:halo + lc, :] = c_ref[...].astype(F32)

    dt_all = _softplus(dt_ref[...] + dtb_ref[...][None]).reshape(n_chunks * hpg, q_len)
    a_all = jnp.broadcast_to(-jnp.exp(alog_ref[...])[None],
                             (n_chunks, hpg, 1)).reshape(n_chunks * hpg, 1)
    da3 = jnp.concatenate(_split_terms(dt_all * a_all, 3), axis=1).astype(BF16)
    cum2 = jnp.dot(da3, u3_ref[...], preferred_element_type=F32) * LOG2_E
    cum2_last = cum2[:, q_len - 1:q_len]
    te_all = jnp.exp2(cum2_last - cum2) * dt_all
    cumj_scr[...] = (cum2 - jnp.log2(dt_all)).reshape(n_chunks, hpg, q_len)
    pieces = [p.reshape(n_chunks, hpg, q_len)
              for p in _split_terms(cum2, 3) + _split_terms(te_all, 2)]
    pieces.append(jnp.zeros((n_chunks, hpg, q_len), F32))
    lhs_scr[...] = jnp.concatenate(pieces, axis=1).astype(BF16)
    dl_scr[...] = jnp.sum(jnp.exp2(cum2_last).reshape(n_chunks, hpg, 1) * rep_ref[...][None],
                          axis=1)

    def conv(k):
        r0 = k * q_len if isinstance(k, int) else pl.multiple_of(k * q_len, q_len)
        for s in range(ns):
            lanes = slice(s * LANES, (s + 1) * LANES)
            taps = {d: ubuf.at[s][pl.ds(r0 + halo + d, pr, stride=CONV_PHASES), :]
                    for d in range(1 - CONV_WIDTH, CONV_PHASES)}
            for ph in range(CONV_PHASES):
                acc = cb_ref[:, lanes]
                for kk in range(CONV_WIDTH):
                    acc = acc + cw_ref[kk:kk + 1, lanes] * taps[ph - (CONV_WIDTH - 1) + kk]
                xc.at[s][pl.ds(r0 + ph, pr, stride=CONV_PHASES), :] = _silu_of_twice(acc)

    dsk = dsk_ref[...]
    ng = ng_ref[...]
    row_i = lax.broadcasted_iota(jnp.int32, (q_len, q_len), 0)
    col_j = lax.broadcasted_iota(jnp.int32, (q_len, q_len), 1)
    causal = row_i >= col_j
    low_half = lax.broadcasted_iota(jnp.int32, (1, LANES), 1) < SSM_HEAD_DIM

    def chunk_rows(k):
        start = k * q_len
        return pl.ds(start if isinstance(k, int) else pl.multiple_of(start, q_len), q_len)

    def load_x(rows):
        return jnp.concatenate([xc[s, rows, :] for s in range(ns_x)], axis=1)

    def prep(k, slot):
        rows = chunk_rows(k)
        xq = load_x(rows)
        cq = xc[ns_x + 1, rows, :]
        bq16 = xc[ns_x, rows, :].astype(BF16)
        cq16 = cq.astype(BF16)

        xp = lax.dot_general(lhs_scr[k], ex_ref[...], (((0,), (0,)), ((), ())),
                             preferred_element_type=F32)
        ci_scr[slot] = xp[:, 0:hpg * q_len]
        te_exp = xp[:, hpg * q_len:hpg * q_len + gw]

        cb = lax.dot_general(cq16, bq16, (((1,), (1,)), ((), ())),
                             preferred_element_type=F32)
        g_scr[slot] = jnp.concatenate([cb, cq], axis=1)

        xte = (xq * te_exp).astype(BF16)
        s_scr[slot] = lax.dot_general(bq16, xte, (((0,), (0,)), ((), ())),
                                      preferred_element_type=F32)

    def heads(k, slot):
        rows = chunk_rows(k)
        xq = load_x(rows)
        g_mat = g_scr[slot]
        cum_j = cumj_scr[k]
        st = state_scr[...]
        rhs = jnp.concatenate([xq.astype(BF16), st.astype(BF16)], axis=0)
        ys = []
        for m in range(hpg // 2):
            rp = rhs[:, m * LANES:(m + 1) * LANES]
            parts = []
            for r in (2 * m, 2 * m + 1):
                ci = ci_scr[slot, :, r * q_len:(r + 1) * q_len]
                seg = jnp.where(causal, ci - cum_j[r:r + 1, :], -jnp.inf)
                lhs = (g_mat * jnp.exp2(jnp.concatenate([seg, ci], axis=1))).astype(BF16)
                parts.append(jnp.dot(lhs, rp, preferred_element_type=F32))
            ys.append(jnp.where(low_half, parts[0], parts[1]))
        y_mm = jnp.concatenate(ys, axis=1)

        state_scr[...] = st * dl_scr[pl.ds(k, 1), :] + s_scr[slot]

        z_half = z_ref[rows, :].astype(F32)
        y = (y_mm + dsk * xq) * _silu_of_twice(z_half)
        y_ref[rows, :] = _rms(y, ng).astype(y_ref.dtype)

    conv(0)
    conv(1)
    prep(0, 0)

    def pair(kk, carry):
        k0 = 2 * kk
        heads(k0, 0)
        prep(k0 + 1, 1)
        conv(k0 + 2)
        heads(k0 + 1, 1)
        prep(k0 + 2, 0)
        conv(k0 + 3)
        return carry

    lax.fori_loop(0, n_chunks // 2 - 1, pair, 0)
    heads(n_chunks - 2, 0)
    prep(n_chunks - 1, 1)
    heads(n_chunks - 1, 1)
    ubuf[:, 0:halo, :] = ubuf[:, lc:lc + halo, :]


def _ssd(proj, dt_t, cw, cb, dtb, alog, dsk, ng, *, batch, seq, d_inner, lc=2048):
    t = proj.shape[0]
    groups = SSM_GROUPS
    gw = d_inner // groups
    hpg = gw // SSM_HEAD_DIM
    q_len = SSD_Q
    n_tiles = seq // lc
    n_chunks = lc // q_len
    cwid = gw + 2 * D_STATE
    ns = cwid // LANES
    assert D_STATE == q_len == LANES and hpg % 2 == 0 and gw % LANES == 0
    xs_blk0 = d_inner // gw
    b_blk0 = 2 * d_inner // D_STATE
    c_blk0 = b_blk0 + groups

    tri = np.triu(np.ones((q_len, q_len), np.float32))
    u3 = jnp.asarray(np.concatenate([tri, tri, tri], axis=0), BF16)
    rep = np.kron(np.eye(hpg, dtype=np.float32), np.ones((1, SSM_HEAD_DIM), np.float32))
    rq = np.kron(np.eye(hpg, dtype=np.float32), np.ones((1, q_len), np.float32))
    zq = np.zeros_like(rq)
    zr = np.zeros_like(rep)
    ex = np.concatenate([
        np.concatenate([rq, zr], axis=1), np.concatenate([rq, zr], axis=1),
        np.concatenate([rq, zr], axis=1),
        np.concatenate([zq, rep], axis=1), np.concatenate([zq, rep], axis=1),
        np.concatenate([zq, zr], axis=1),
    ], axis=0)
    ex = jnp.asarray(ex, BF16)
    rep = jnp.asarray(rep, F32)

    row = lambda b, g, c: b * n_tiles + c
    const2 = lambda b, g, c: (0, 0)
    per_group = lambda b, g, c: (g, 0, 0)
    return pl.pallas_call(
        functools.partial(_ssd_kernel, lc=lc, gw=gw),
        grid=(batch, groups, n_tiles),
        in_specs=[
            pl.BlockSpec((lc, gw), lambda b, g, c: (row(b, g, c), g)),
            pl.BlockSpec((lc, gw), lambda b, g, c: (row(b, g, c), xs_blk0 + g)),
            pl.BlockSpec((lc, D_STATE), lambda b, g, c: (row(b, g, c), b_blk0 + g)),
            pl.BlockSpec((lc, D_STATE), lambda b, g, c: (row(b, g, c), c_blk0 + g)),
            pl.BlockSpec((lc // q_len, hpg, q_len), lambda b, g, c: (row(b, g, c), g, 0)),
            pl.BlockSpec((None, CONV_WIDTH, cwid), per_group),
            pl.BlockSpec((None, 1, cwid), per_group),
            pl.BlockSpec((None, hpg, 1), per_group),
            pl.BlockSpec((None, hpg, 1), per_group),
            pl.BlockSpec((None, 1, gw), per_group),
            pl.BlockSpec((None, 1, gw), per_group),
            pl.BlockSpec(u3.shape, const2),
            pl.BlockSpec(ex.shape, const2),
            pl.BlockSpec(rep.shape, const2),
        ],
        out_specs=pl.BlockSpec((lc, gw), lambda b, g, c: (row(b, g, c), g)),
        out_shape=jax.ShapeDtypeStruct((t, d_inner), BF16),
        scratch_shapes=[
            pltpu.VMEM((D_STATE, gw), F32),
            pltpu.VMEM((ns, lc + 8, LANES), F32),
            pltpu.VMEM((ns, lc, LANES), F32),
            pltpu.VMEM((n_chunks, 6 * hpg, q_len), BF16),
            pltpu.VMEM((n_chunks, hpg, q_len), F32),
            pltpu.VMEM((n_chunks, gw), F32),
            pltpu.VMEM((2, q_len, hpg * q_len), F32),
            pltpu.VMEM((2, q_len, q_len + D_STATE), F32),
            pltpu.VMEM((2, D_STATE, gw), F32),
        ],
        compiler_params=_cparams(("parallel", "parallel", "arbitrary"), 44),
        name="ssd",
    )(proj, proj, proj, proj, dt_t, cw, cb, dtb, alog, dsk, ng, u3, ex, rep)


def _pool_kernel(u_ref, halo_ref, wg_ref, sc_ref, o_ref, buf, *, tp, tiles_per_seq, gdim):
    i = pl.program_id(0)
    seq_tile = i % tiles_per_seq
    keep = (seq_tile != 0).astype(F32)
    buf[0:POOL_HALO, :] = halo_ref[...].astype(F32) * keep
    buf[POOL_HALO:POOL_HALO + tp, :] = u_ref[...].astype(F32)

    rc = 128
    for r0 in range(0, tp, rc):
        pos = (seq_tile * tp + r0 + 1
               + lax.broadcasted_iota(jnp.int32, (rc, 1), 0)).astype(F32)
        for gi, win in enumerate(POOL_WINDOWS):
            cols = slice(gi * gdim, (gi + 1) * gdim)
            base = POOL_HALO + r0
            cur = buf[base:base + rc, cols]
            s = cur
            for k in range(1, win):
                s = s + buf[base - k:base - k + rc, cols]
            pooled = (s / jnp.minimum(pos, float(win)) - cur).astype(BF16)
            mixed = jnp.dot(pooled, wg_ref[gi], preferred_element_type=F32)
            o_ref[r0:r0 + rc, cols] = (mixed * sc_ref[:, cols]).astype(o_ref.dtype)


def _pool(proj, w_grp, scale, *, seq, d_pool, col0, tp=512):
    t = proj.shape[0]
    n_groups = len(POOL_WINDOWS)
    gdim = d_pool // n_groups
    tiles_per_seq = seq // tp
    cblk = col0 // d_pool
    hb = tp // POOL_HALO
    return pl.pallas_call(
        functools.partial(_pool_kernel, tp=tp, tiles_per_seq=tiles_per_seq, gdim=gdim),
        grid=(t // tp,),
        in_specs=[
            pl.BlockSpec((tp, d_pool), lambda i: (i, cblk)),
            pl.BlockSpec((POOL_HALO, d_pool), lambda i: (jnp.maximum(i * hb - 1, 0), cblk)),
            pl.BlockSpec((n_groups, gdim, gdim), lambda i: (0, 0, 0)),
            pl.BlockSpec((1, d_pool), lambda i: (0, 0)),
        ],
        out_specs=pl.BlockSpec((tp, d_pool), lambda i: (i, 0)),
        out_shape=jax.ShapeDtypeStruct((t, d_pool), BF16),
        scratch_shapes=[pltpu.VMEM((tp + POOL_HALO, d_pool), F32)],
        compiler_params=_cparams(("parallel",), 32),
        name="pool",
    )(proj, proj, w_grp, scale)


def _merge_kernel(ys_ref, yp_ref, gs_ref, gp_ref, ws_ref, wp_ref, o_ref):
    a = jnp.dot(ys_ref[...], ws_ref[...], preferred_element_type=F32)
    b = jnp.dot(yp_ref[...], wp_ref[...], preferred_element_type=F32)
    merged = (jax.nn.sigmoid(gs_ref[...].astype(F32)) * a
              + jax.nn.sigmoid(gp_ref[...].astype(F32)) * b)
    o_ref[...] = merged.astype(o_ref.dtype)


def _merge(y_ssm, y_pool, proj, w_so, w_po, *, gs_col0, gp_col0, tm=1024, tn=512):
    t, d_inner = y_ssm.shape
    d_pool = y_pool.shape[1]
    d = w_so.shape[1]
    gs_blk = gs_col0 // tn
    gp_blk = gp_col0 // tn
    return pl.pallas_call(
        _merge_kernel,
        grid=(t // tm, d // tn),
        in_specs=[
            pl.BlockSpec((tm, d_inner), lambda i, j: (i, 0)),
            pl.BlockSpec((tm, d_pool), lambda i, j: (i, 0)),
            pl.BlockSpec((tm, tn), lambda i, j: (i, gs_blk + j)),
            pl.BlockSpec((tm, tn), lambda i, j: (i, gp_blk + j)),
            pl.BlockSpec((d_inner, tn), lambda i, j: (0, j)),
            pl.BlockSpec((d_pool, tn), lambda i, j: (0, j)),
        ],
        out_specs=pl.BlockSpec((tm, tn), lambda i, j: (i, j)),
        out_shape=jax.ShapeDtypeStruct((t, d), BF16),
        compiler_params=_cparams(("parallel", "arbitrary"), 48),
        name="merge",
    )(y_ssm, y_pool, proj, proj, w_so, w_po)


def _mix_kernel(m_ref, x_ref, w_ref, gpost_ref, gpre_ref, x1_ref, h2_ref):
    for r0 in range(0, m_ref.shape[0], ROW_SUB):
        rows = slice(r0, r0 + ROW_SUB)
        mix = jnp.dot(m_ref[rows, :], w_ref[...], preferred_element_type=F32)
        x1 = x_ref[rows, :] + _rms(mix, gpost_ref[...])
        x1_ref[rows, :] = x1
        h2_ref[rows, :] = _rms(x1, gpre_ref[...]).astype(h2_ref.dtype)


def _mix(merged, x2d, w_mix, g_post, g_pre, *, tm=512):
    t, d = x2d.shape
    return pl.pallas_call(
        _mix_kernel,
        grid=(t // tm,),
        in_specs=[
            pl.BlockSpec((tm, d), lambda i: (i, 0)),
            pl.BlockSpec((tm, d), lambda i: (i, 0)),
            pl.BlockSpec((d, d), lambda i: (0, 0)),
            pl.BlockSpec((1, d), lambda i: (0, 0)),
            pl.BlockSpec((1, d), lambda i: (0, 0)),
        ],
        out_specs=[
            pl.BlockSpec((tm, d), lambda i: (i, 0)),
            pl.BlockSpec((tm, d), lambda i: (i, 0)),
        ],
        out_shape=[
            jax.ShapeDtypeStruct((t, d), F32),
            jax.ShapeDtypeStruct((t, d), BF16),
        ],
        compiler_params=_cparams(("parallel",), 48),
        name="mix",
    )(merged, x2d, w_mix, g_post, g_pre)


def _ffn_up_kernel(h_ref, wg_ref, wu_ref, o_ref):
    h = h_ref[...]
    gate = jnp.dot(h, wg_ref[...], preferred_element_type=F32)
    up = jnp.dot(h, wu_ref[...], preferred_element_type=F32)
    o_ref[...] = (_silu(gate) * up).astype(o_ref.dtype)


def _ffn_up(h2, w_ffn_in, *, tm=1024, tf=512):
    t, d = h2.shape
    d_ff = w_ffn_in.shape[1] // 2
    n_f = d_ff // tf
    return pl.pallas_call(
        _ffn_up_kernel,
        grid=(t // tm, n_f),
        in_specs=[
            pl.BlockSpec((tm, d), lambda i, j: (i, 0)),
            pl.BlockSpec((d, tf), lambda i, j: (0, j)),
            pl.BlockSpec((d, tf), lambda i, j: (0, n_f + j)),
        ],
        out_specs=pl.BlockSpec((tm, tf), lambda i, j: (i, j)),
        out_shape=jax.ShapeDtypeStruct((t, d_ff), BF16),
        compiler_params=_cparams(("parallel", "arbitrary"), 40),
        name="ffn_up",
    )(h2, w_ffn_in, w_ffn_in)


def _ffn_down_kernel(a_ref, wo_ref, o_ref):
    o_ref[...] = jnp.dot(a_ref[...], wo_ref[...], preferred_element_type=F32)


def _ffn_down(act, w_out, *, tm=1024, tn=512):
    t, d_ff = act.shape
    d = w_out.shape[1]
    return pl.pallas_call(
        _ffn_down_kernel,
        grid=(t // tm, d // tn),
        in_specs=[
            pl.BlockSpec((tm, d_ff), lambda i, j: (i, 0)),
            pl.BlockSpec((d_ff, tn), lambda i, j: (0, j)),
        ],
        out_specs=pl.BlockSpec((tm, tn), lambda i, j: (i, j)),
        out_shape=jax.ShapeDtypeStruct((t, d), F32),
        compiler_params=_cparams(("parallel", "arbitrary"), 48),
        name="ffn_down",
    )(act, w_out)


def _ple_kernel(f_ref, x1_ref, p_ref, g_ref, wg_ref, wp_ref, o_ref):
    for r0 in range(0, f_ref.shape[0], ROW_SUB):
        rows = slice(r0, r0 + ROW_SUB)
        x2 = x1_ref[rows, :] + _rms(f_ref[rows, :], g_ref[...])
        gate = jnp.dot(x2.astype(BF16), wg_ref[...], preferred_element_type=F32)
        ple = jnp.dot(p_ref[rows, :].astype(BF16), wp_ref[...], preferred_element_type=F32)
        o_ref[rows, :] = x2 + jax.nn.sigmoid(gate) * ple


def _ple(f, x1, p2d, g_post, w_gate, w_proj, *, tm=512):
    t, d = x1.shape
    k = p2d.shape[1]
    return pl.pallas_call(
        _ple_kernel,
        grid=(t // tm,),
        in_specs=[
            pl.BlockSpec((tm, d), lambda i: (i, 0)),
            pl.BlockSpec((tm, d), lambda i: (i, 0)),
            pl.BlockSpec((tm, k), lambda i: (i, 0)),
            pl.BlockSpec((1, d), lambda i: (0, 0)),
            pl.BlockSpec((d, d), lambda i: (0, 0)),
            pl.BlockSpec((k, d), lambda i: (0, 0)),
        ],
        out_specs=pl.BlockSpec((tm, d), lambda i: (i, 0)),
        out_shape=jax.ShapeDtypeStruct((t, d), F32),
        compiler_params=_cparams(("parallel",), 52),
        name="ple",
    )(f, x1, p2d, g_post, w_gate, w_proj)


def kernel(x, p, w_in, conv_w, conv_b, dt_bias, a_log, d_skip, ssm_norm_g, w_ssm_out,
           w_pool_grp, pool_scale, w_pool_out, w_mix_out, g_pre_mix, g_post_mix,
           g_pre_ffn, g_post_ffn, w_ffn_in, w_ffn_out, w_ple_proj, w_ple_gate):
    batch, seq, d = x.shape
    depth = w_in.shape[0]
    d_inner = w_ssm_out.shape[1]
    d_pool = w_pool_out.shape[1]
    heads = dt_bias.shape[1]
    groups = SSM_GROUPS
    hpg = heads // groups
    gw = d_inner // groups
    n_bc = groups * D_STATE
    conv_dim = d_inner + 2 * n_bc
    t = batch * seq
    assert heads * SSM_HEAD_DIM == d_inner and conv_w.shape[2] == conv_dim
    assert w_in.shape[2] == d_inner + conv_dim + heads + d_pool + 2 * d

    dt_col0 = d_inner + conv_dim
    pool_col0 = dt_col0
    gs_col0 = pool_col0 + d_pool
    gp_col0 = gs_col0 + d

    x2d = x.reshape(t, d)
    for li in range(depth):
        wi = w_in[li]
        w_t = wi.T.astype(BF16)

        def per_group_cols(v):
            xs = v[..., :d_inner].reshape(v.shape[:-1] + (groups, gw))
            bm = v[..., d_inner:d_inner + n_bc].reshape(v.shape[:-1] + (groups, D_STATE))
            cm = v[..., d_inner + n_bc:].reshape(v.shape[:-1] + (groups, D_STATE))
            cat = jnp.concatenate([xs, bm, cm], axis=-1)
            return jnp.moveaxis(cat, -2, 0)
        cw = per_group_cols(0.5 * conv_w[li])
        cb = per_group_cols(0.5 * conv_b[li][None, :])
        dtb = dt_bias[li].reshape(groups, hpg, 1)
        alog = a_log[li].reshape(groups, hpg, 1)
        dsk = jnp.repeat(d_skip[li], SSM_HEAD_DIM).reshape(groups, 1, gw)
        ng = ssm_norm_g[li].reshape(groups, 1, gw)

        proj, dt_t = _in_proj(x2d, g_pre_mix[li][None, :], w_t, dt_row0=dt_col0,
                              n_half_cols=d_inner)
        y_ssm = _ssd(proj, dt_t, cw, cb, dtb, alog, dsk, ng,
                     batch=batch, seq=seq, d_inner=d_inner)
        y_pool = _pool(proj, w_pool_grp[li].astype(BF16), pool_scale[li][None, :],
                       seq=seq, d_pool=d_pool, col0=pool_col0)
        merged = _merge(y_ssm, y_pool, proj, w_ssm_out[li].astype(BF16),
                        w_pool_out[li].astype(BF16), gs_col0=gs_col0, gp_col0=gp_col0)
        x1, h2 = _mix(merged, x2d, w_mix_out[li].astype(BF16),
                      g_post_mix[li][None, :], g_pre_ffn[li][None, :])
        act = _ffn_up(h2, w_ffn_in[li].astype(BF16))
        f = _ffn_down(act, w_ffn_out[li].astype(BF16))
        x2d = _ple(f, x1, p[li].reshape(t, -1), g_post_ffn[li][None, :],
                   w_ple_gate[li].astype(BF16), w_ple_proj[li].astype(BF16))
    return x2d.reshape(batch, seq, d)
```

```python
import functools

import jax
import jax.numpy as jnp
import numpy as np
from jax import lax
from jax.experimental import pallas as pl
from jax.experimental.pallas import tpu as pltpu

F32 = jnp.float32
BF16 = jnp.bfloat16

EPS = 1e-6
SSM_HEAD_DIM = 64
SSM_GROUPS = 8
D_STATE = 128
CONV_WIDTH = 4
POOL_WINDOWS = (2, 4, 8, 16)
POOL_HALO = 16
SSD_Q = 128
LANES = 128
CONV_PHASES = 4
ROW_SUB = 256

LOG2_E = 1.4426950408889634

MIB = 1024 * 1024


def _cparams(semantics, vmem_mib):
    return pltpu.CompilerParams(dimension_semantics=semantics,
                                vmem_limit_bytes=vmem_mib * MIB)


def _rms(v, g):
    ms = jnp.mean(v * v, axis=-1, keepdims=True)
    return v * lax.rsqrt(ms + EPS) * g


def _silu(v):
    return v * jax.nn.sigmoid(v)


def _silu_of_twice(h):
    return h + h * jnp.tanh(h)


def _softplus(v):
    return jnp.maximum(v, 0.0) + jnp.log1p(jnp.exp(-jnp.abs(v)))


def _split_terms(v, n_terms):
    pieces = []
    rem = v
    for _ in range(n_terms - 1):
        hi = rem.astype(BF16).astype(F32)
        pieces.append(hi)
        rem = rem - hi
    pieces.append(rem)
    return pieces


def _in_proj_kernel(x_ref, g_ref, w_ref, wdt_ref, o_ref, dt_ref, h_scr, *, n_chunks, n_half):
    @pl.when(pl.program_id(1) == 0)
    def _():
        def body(q, carry):
            rows = pl.ds(pl.multiple_of(q * SSD_Q, SSD_Q), SSD_Q)
            h = _rms(x_ref[rows, :], g_ref[...]).astype(BF16)
            h_scr[rows, :] = h
            dt_ref[q] = lax.dot_general(wdt_ref[...], h, (((1,), (1,)), ((), ())),
                                        preferred_element_type=F32)
            return carry
        lax.fori_loop(0, n_chunks, body, 0, unroll=2)

    acc = lax.dot_general(h_scr[...], w_ref[...], (((1,), (1,)), ((), ())),
                          preferred_element_type=F32)
    scale = jnp.where(pl.program_id(1) < n_half, 0.5, 1.0).astype(F32)
    o_ref[...] = (acc * scale).astype(o_ref.dtype)


def _in_proj(x2d, g, w_t, *, dt_row0, n_half_cols, tm=1024, tn=1024):
    t, d = x2d.shape
    heads = n_half_cols // SSM_HEAD_DIM
    n = w_t.shape[0] - heads
    n_chunks = tm // SSD_Q
    n_before = dt_row0 // tn
    n_half = n_half_cols // tn
    assert dt_row0 % tn == 0 and n_half_cols % tn == 0 and dt_row0 % heads == 0
    return pl.pallas_call(
        functools.partial(_in_proj_kernel, n_chunks=n_chunks, n_half=n_half),
        grid=(t // tm, n // tn),
        in_specs=[
            pl.BlockSpec((tm, d), lambda i, j: (i, 0)),
            pl.BlockSpec((1, d), lambda i, j: (0, 0)),
            pl.BlockSpec((pl.Element(tn), pl.Element(d)),
                         lambda i, j: ((j * (tn // heads) + (j >= n_before).astype(jnp.int32))
                                       * heads, 0)),
            pl.BlockSpec((heads, d), lambda i, j: (dt_row0 // heads, 0)),
        ],
        out_specs=[
            pl.BlockSpec((tm, tn), lambda i, j: (i, j)),
            pl.BlockSpec((n_chunks, heads, SSD_Q), lambda i, j: (i, 0, 0)),
        ],
        out_shape=[
            jax.ShapeDtypeStruct((t, n), BF16),
            jax.ShapeDtypeStruct((t // SSD_Q, heads, SSD_Q), F32),
        ],
        scratch_shapes=[pltpu.VMEM((tm, d), BF16)],
        compiler_params=_cparams(("parallel", "arbitrary"), 48),
        name="in_proj",
    )(x2d, g, w_t, w_t)


def _ssd_kernel(z_ref, xs_ref, b_ref, c_ref, dt_ref, cw_ref, cb_ref, dtb_ref, alog_ref,
                dsk_ref, ng_ref, u3_ref, ex_ref, rep_ref, y_ref,
                state_scr, ubuf, xc, lhs_scr, cumj_scr, dl_scr, ci_scr, g_scr, s_scr,
                *, lc, gw):
    q_len = SSD_Q
    n_chunks = lc // q_len
    hpg = gw // SSM_HEAD_DIM
    ns_x = gw // LANES
    ns = ns_x + 2
    halo = 8
    pr = q_len // CONV_PHASES

    @pl.when(pl.program_id(2) == 0)
    def _():
        state_scr[...] = jnp.zeros_like(state_scr)
        ubuf[:, 0:halo, :] = jnp.zeros((ns, halo, LANES), F32)

    for s in range(ns_x):
        ubuf[s, halo:halo + lc, :] = xs_ref[:, s * LANES:(s + 1) * LANES].astype(F32)
    ubuf[ns_x, halo:halo + lc, :] = b_ref[...].astype(F32)
    ubuf[ns_x + 1, halo:halo + lc, :] = c_ref[...].astype(F32)

    dt_all = _softplus(dt_ref[...] + dtb_ref[...][None]).reshape(n_chunks * hpg, q_len)
    a_all = jnp.broadcast_to(-jnp.exp(alog_ref[...])[None],
                             (n_chunks, hpg, 1)).reshape(n_chunks * hpg, 1)
    da3 = jnp.concatenate(_split_terms(dt_all * a_all, 3), axis=1).astype(BF16)
    cum2 = jnp.dot(da3, u3_ref[...], preferred_element_type=F32) * LOG2_E
    cum2_last = cum2[:, q_len - 1:q_len]
    te_all = jnp.exp2(cum2_last - cum2) * dt_all
    cumj_scr[...] = (cum2 - jnp.log2(dt_all)).reshape(n_chunks, hpg, q_len)
    pieces = [p.reshape(n_chunks, hpg, q_len)
              for p in _split_terms(cum2, 3) + _split_terms(te_all, 2)]
    pieces.append(jnp.zeros((n_chunks, hpg, q_len), F32))
    lhs_scr[...] = jnp.concatenate(pieces, axis=1).astype(BF16)
    dl_scr[...] = jnp.sum(jnp.exp2(cum2_last).reshape(n_chunks, hpg, 1) * rep_ref[...][None],
                          axis=1)

    def conv(k):
        r0 = k * q_len if isinstance(k, int) else pl.multiple_of(k * q_len, q_len)
        for s in range(ns):
            lanes = slice(s * LANES, (s + 1) * LANES)
            taps = {d: ubuf.at[s][pl.ds(r0 + halo + d, pr, stride=CONV_PHASES), :]
                    for d in range(1 - CONV_WIDTH, CONV_PHASES)}
            for ph in range(CONV_PHASES):
                acc = cb_ref[:, lanes]
                for kk in range(CONV_WIDTH):
                    acc = acc + cw_ref[kk:kk + 1, lanes] * taps[ph - (CONV_WIDTH - 1) + kk]
                xc.at[s][pl.ds(r0 + ph, pr, stride=CONV_PHASES), :] = _silu_of_twice(acc)

    dsk = dsk_ref[...]
    ng = ng_ref[...]
    row_i = lax.broadcasted_iota(jnp.int32, (q_len, q_len), 0)
    col_j = lax.broadcasted_iota(jnp.int32, (q_len, q_len), 1)
    causal = row_i >= col_j
    low_half = lax.broadcasted_iota(jnp.int32, (1, LANES), 1) < SSM_HEAD_DIM

    def chunk_rows(k):
        start = k * q_len
        return pl.ds(start if isinstance(k, int) else pl.multiple_of(start, q_len), q_len)

    def load_x(rows):
        return jnp.concatenate([xc[s, rows, :] for s in range(ns_x)], axis=1)

    def prep(k, slot):
        rows = chunk_rows(k)
        xq = load_x(rows)
        cq = xc[ns_x + 1, rows, :]
        bq16 = xc[ns_x, rows, :].astype(BF16)
        cq16 = cq.astype(BF16)

        xp = lax.dot_general(lhs_scr[k], ex_ref[...], (((0,), (0,)), ((), ())),
                             preferred_element_type=F32)
        ci_scr[slot] = xp[:, 0:hpg * q_len]
        te_exp = xp[:, hpg * q_len:hpg * q_len + gw]

        cb = lax.dot_general(cq16, bq16, (((1,), (1,)), ((), ())),
                             preferred_element_type=F32)
        g_scr[slot] = jnp.concatenate([cb, cq], axis=1)

        xte = (xq * te_exp).astype(BF16)
        s_scr[slot] = lax.dot_general(bq16, xte, (((0,), (0,)), ((), ())),
                                      preferred_element_type=F32)

    def heads(k, slot):
        rows = chunk_rows(k)
        xq = load_x(rows)
        g_mat = g_scr[slot]
        cum_j = cumj_scr[k]
        st = state_scr[...]
        rhs = jnp.concatenate([xq.astype(BF16), st.astype(BF16)], axis=0)
        ys = []
        for m in range(hpg // 2):
            rp = rhs[:, m * LANES:(m + 1) * LANES]
            parts = []
            for r in (2 * m, 2 * m + 1):
                ci = ci_scr[slot, :, r * q_len:(r + 1) * q_len]
                seg = jnp.where(causal, ci - cum_j[r:r + 1, :], -jnp.inf)
                lhs = (g_mat * jnp.exp2(jnp.concatenate([seg, ci], axis=1))).astype(BF16)
                parts.append(jnp.dot(lhs, rp, preferred_element_type=F32))
            ys.append(jnp.where(low_half, parts[0], parts[1]))
        y_mm = jnp.concatenate(ys, axis=1)

        state_scr[...] = st * dl_scr[pl.ds(k, 1), :] + s_scr[slot]

        z_half = z_ref[rows, :].astype(F32)
        y = (y_mm + dsk * xq) * _silu_of_twice(z_half)
        y_ref[rows, :] = _rms(y, ng).astype(y_ref.dtype)

    conv(0)
    conv(1)
    prep(0, 0)

    def pair(kk, carry):
        k0 = 2 * kk
        heads(k0, 0)
        prep(k0 + 1, 1)
        conv(k0 + 2)
        heads(k0 + 1, 1)
        prep(k0 + 2, 0)
        conv(k0 + 3)
        return carry

    lax.fori_loop(0, n_chunks // 2 - 1, pair, 0)
    heads(n_chunks - 2, 0)
    prep(n_chunks - 1, 1)
    heads(n_chunks - 1, 1)
    ubuf[:, 0:halo, :] = ubuf[:, lc:lc + halo, :]


def _ssd(proj, dt_t, cw, cb, dtb, alog, dsk, ng, *, batch, seq, d_inner, lc=2048):
    t = proj.shape[0]
    groups = SSM_GROUPS
    gw = d_inner // groups
    hpg = gw // SSM_HEAD_DIM
    q_len = SSD_Q
    n_tiles = seq // lc
    n_chunks = lc // q_len
    cwid = gw + 2 * D_STATE
    ns = cwid // LANES
    assert D_STATE == q_len == LANES and hpg % 2 == 0 and gw % LANES == 0
    xs_blk0 = d_inner // gw
    b_blk0 = 2 * d_inner // D_STATE
    c_blk0 = b_blk0 + groups

    tri = np.triu(np.ones((q_len, q_len), np.float32))
    u3 = jnp.asarray(np.concatenate([tri, tri, tri], axis=0), BF16)
    rep = np.kron(np.eye(hpg, dtype=np.float32), np.ones((1, SSM_HEAD_DIM), np.float32))
    rq = np.kron(np.eye(hpg, dtype=np.float32), np.ones((1, q_len), np.float32))
    zq = np.zeros_like(rq)
    zr = np.zeros_like(rep)
    ex = np.concatenate([
        np.concatenate([rq, zr], axis=1), np.concatenate([rq, zr], axis=1),
        np.concatenate([rq, zr], axis=1),
        np.concatenate([zq, rep], axis=1), np.concatenate([zq, rep], axis=1),
        np.concatenate([zq, zr], axis=1),
    ], axis=0)
    ex = jnp.asarray(ex, BF16)
    rep = jnp.asarray(rep, F32)

    row = lambda b, g, c: b * n_tiles + c
    const2 = lambda b, g, c: (0, 0)
    per_group = lambda b, g, c: (g, 0, 0)
    return pl.pallas_call(
        functools.partial(_ssd_kernel, lc=lc, gw=gw),
        grid=(batch, groups, n_tiles),
        in_specs=[
            pl.BlockSpec((lc, gw), lambda b, g, c: (row(b, g, c), g)),
            pl.BlockSpec((lc, gw), lambda b, g, c: (row(b, g, c), xs_blk0 + g)),
            pl.BlockSpec((lc, D_STATE), lambda b, g, c: (row(b, g, c), b_blk0 + g)),
            pl.BlockSpec((lc, D_STATE), lambda b, g, c: (row(b, g, c), c_blk0 + g)),
            pl.BlockSpec((lc // q_len, hpg, q_len), lambda b, g, c: (row(b, g, c), g, 0)),
            pl.BlockSpec((None, CONV_WIDTH, cwid), per_group),
            pl.BlockSpec((None, 1, cwid), per_group),
            pl.BlockSpec((None, hpg, 1), per_group),
            pl.BlockSpec((None, hpg, 1), per_group),
            pl.BlockSpec((None, 1, gw), per_group),
            pl.BlockSpec((None, 1, gw), per_group),
            pl.BlockSpec(u3.shape, const2),
            pl.BlockSpec(ex.shape, const2),
            pl.BlockSpec(rep.shape, const2),
        ],
        out_specs=pl.BlockSpec((lc, gw), lambda b, g, c: (row(b, g, c), g)),
        out_shape=jax.ShapeDtypeStruct((t, d_inner), BF16),
        scratch_shapes=[
            pltpu.VMEM((D_STATE, gw), F32),
            pltpu.VMEM((ns, lc + 8, LANES), F32),
            pltpu.VMEM((ns, lc, LANES), F32),
            pltpu.VMEM((n_chunks, 6 * hpg, q_len), BF16),
            pltpu.VMEM((n_chunks, hpg, q_len), F32),
            pltpu.VMEM((n_chunks, gw), F32),
            pltpu.VMEM((2, q_len, hpg * q_len), F32),
            pltpu.VMEM((2, q_len, q_len + D_STATE), F32),
            pltpu.VMEM((2, D_STATE, gw), F32),
        ],
        compiler_params=_cparams(("parallel", "parallel", "arbitrary"), 44),
        name="ssd",
    )(proj, proj, proj, proj, dt_t, cw, cb, dtb, alog, dsk, ng, u3, ex, rep)


def _pool_kernel(u_ref, halo_ref, wg_ref, sc_ref, o_ref, buf, *, tp, tiles_per_seq, gdim):
    i = pl.program_id(0)
    seq_tile = i % tiles_per_seq
    keep = (seq_tile != 0).astype(F32)
    buf[0:POOL_HALO, :] = halo_ref[...].astype(F32) * keep
    buf[POOL_HALO:POOL_HALO + tp, :] = u_ref[...].astype(F32)

    rc = 128
    for r0 in range(0, tp, rc):
        pos = (seq_tile * tp + r0 + 1
               + lax.broadcasted_iota(jnp.int32, (rc, 1), 0)).astype(F32)
        for gi, win in enumerate(POOL_WINDOWS):
            cols = slice(gi * gdim, (gi + 1) * gdim)
            base = POOL_HALO + r0
            cur = buf[base:base + rc, cols]
            s = cur
            for k in range(1, win):
                s = s + buf[base - k:base - k + rc, cols]
            pooled = (s / jnp.minimum(pos, float(win)) - cur).astype(BF16)
            mixed = jnp.dot(pooled, wg_ref[gi], preferred_element_type=F32)
            o_ref[r0:r0 + rc, cols] = (mixed * sc_ref[:, cols]).astype(o_ref.dtype)


def _pool(proj, w_grp, scale, *, seq, d_pool, col0, tp=512):
    t = proj.shape[0]
    n_groups = len(POOL_WINDOWS)
    gdim = d_pool // n_groups
    tiles_per_seq = seq // tp
    cblk = col0 // d_pool
    hb = tp // POOL_HALO
    return pl.pallas_call(
        functools.partial(_pool_kernel, tp=tp, tiles_per_seq=tiles_per_seq, gdim=gdim),
        grid=(t // tp,),
        in_specs=[
            pl.BlockSpec((tp, d_pool), lambda i: (i, cblk)),
            pl.BlockSpec((POOL_HALO, d_pool), lambda i: (jnp.maximum(i * hb - 1, 0), cblk)),
            pl.BlockSpec((n_groups, gdim, gdim), lambda i: (0, 0, 0)),
            pl.BlockSpec((1, d_pool), lambda i: (0, 0)),
        ],
        out_specs=pl.BlockSpec((tp, d_pool), lambda i: (i, 0)),
        out_shape=jax.ShapeDtypeStruct((t, d_pool), BF16),
        scratch_shapes=[pltpu.VMEM((tp + POOL_HALO, d_pool), F32)],
        compiler_params=_cparams(("parallel",), 32),
        name="pool",
    )(proj, proj, w_grp, scale)


def _merge_kernel(ys_ref, yp_ref, gs_ref, gp_ref, ws_ref, wp_ref, o_ref):
    for r0 in range(0, ys_ref.shape[0], 2 * ROW_SUB):
        rows = slice(r0, r0 + 2 * ROW_SUB)
        a = jnp.dot(ys_ref[rows, :], ws_ref[...], preferred_element_type=F32)
        b = jnp.dot(yp_ref[rows, :], wp_ref[...], preferred_element_type=F32)
        merged = (jax.nn.sigmoid(gs_ref[rows, :].astype(F32)) * a
                  + jax.nn.sigmoid(gp_ref[rows, :].astype(F32)) * b)
        o_ref[rows, :] = merged.astype(o_ref.dtype)


def _merge(y_ssm, y_pool, proj, w_so, w_po, *, gs_col0, gp_col0, tm=1024, tn=512):
    t, d_inner = y_ssm.shape
    d_pool = y_pool.shape[1]
    d = w_so.shape[1]
    gs_blk = gs_col0 // tn
    gp_blk = gp_col0 // tn
    return pl.pallas_call(
        _merge_kernel,
        grid=(t // tm, d // tn),
        in_specs=[
            pl.BlockSpec((tm, d_inner), lambda i, j: (i, 0)),
            pl.BlockSpec((tm, d_pool), lambda i, j: (i, 0)),
            pl.BlockSpec((tm, tn), lambda i, j: (i, gs_blk + j)),
            pl.BlockSpec((tm, tn), lambda i, j: (i, gp_blk + j)),
            pl.BlockSpec((d_inner, tn), lambda i, j: (0, j)),
            pl.BlockSpec((d_pool, tn), lambda i, j: (0, j)),
        ],
        out_specs=pl.BlockSpec((tm, tn), lambda i, j: (i, j)),
        out_shape=jax.ShapeDtypeStruct((t, d), BF16),
        compiler_params=_cparams(("parallel", "arbitrary"), 48),
        name="merge",
    )(y_ssm, y_pool, proj, proj, w_so, w_po)


def _mix_kernel(m_ref, x_ref, w_ref, gpost_ref, gpre_ref, x1_ref, h2_ref):
    for r0 in range(0, m_ref.shape[0], ROW_SUB):
        rows = slice(r0, r0 + ROW_SUB)
        mix = jnp.dot(m_ref[rows, :], w_ref[...], preferred_element_type=F32)
        x1 = x_ref[rows, :] + _rms(mix, gpost_ref[...])
        x1_ref[rows, :] = x1
        h2_ref[rows, :] = _rms(x1, gpre_ref[...]).astype(h2_ref.dtype)


def _mix(merged, x2d, w_mix, g_post, g_pre, *, tm=512):
    t, d = x2d.shape
    return pl.pallas_call(
        _mix_kernel,
        grid=(t // tm,),
        in_specs=[
            pl.BlockSpec((tm, d), lambda i: (i, 0)),
            pl.BlockSpec((tm, d), lambda i: (i, 0)),
            pl.BlockSpec((d, d), lambda i: (0, 0)),
            pl.BlockSpec((1, d), lambda i: (0, 0)),
            pl.BlockSpec((1, d), lambda i: (0, 0)),
        ],
        out_specs=[
            pl.BlockSpec((tm, d), lambda i: (i, 0)),
            pl.BlockSpec((tm, d), lambda i: (i, 0)),
        ],
        out_shape=[
            jax.ShapeDtypeStruct((t, d), F32),
            jax.ShapeDtypeStruct((t, d), BF16),
        ],
        compiler_params=_cparams(("parallel",), 48),
        name="mix",
    )(merged, x2d, w_mix, g_post, g_pre)


def _ffn_up_kernel(h_ref, wg_ref, wu_ref, o_ref):
    h = h_ref[...]
    gate = jnp.dot(h, wg_ref[...], preferred_element_type=F32)
    up = jnp.dot(h, wu_ref[...], preferred_element_type=F32)
    o_ref[...] = (_silu(gate) * up).astype(o_ref.dtype)


def _ffn_up(h2, w_ffn_in, *, tm=1024, tf=512):
    t, d = h2.shape
    d_ff = w_ffn_in.shape[1] // 2
    n_f = d_ff // tf
    return pl.pallas_call(
        _ffn_up_kernel,
        grid=(t // tm, n_f),
        in_specs=[
            pl.BlockSpec((tm, d), lambda i, j: (i, 0)),
            pl.BlockSpec((d, tf), lambda i, j: (0, j)),
            pl.BlockSpec((d, tf), lambda i, j: (0, n_f + j)),
        ],
        out_specs=pl.BlockSpec((tm, tf), lambda i, j: (i, j)),
        out_shape=jax.ShapeDtypeStruct((t, d_ff), BF16),
        compiler_params=_cparams(("parallel", "arbitrary"), 40),
        name="ffn_up",
    )(h2, w_ffn_in, w_ffn_in)


def _ffn_down_kernel(a_ref, wo_ref, o_ref):
    o_ref[...] = jnp.dot(a_ref[...], wo_ref[...], preferred_element_type=F32)


def _ffn_down(act, w_out, *, tm=1024, tn=512):
    t, d_ff = act.shape
    d = w_out.shape[1]
    return pl.pallas_call(
        _ffn_down_kernel,
        grid=(t // tm, d // tn),
        in_specs=[
            pl.BlockSpec((tm, d_ff), lambda i, j: (i, 0)),
            pl.BlockSpec((d_ff, tn), lambda i, j: (0, j)),
        ],
        out_specs=pl.BlockSpec((tm, tn), lambda i, j: (i, j)),
        out_shape=jax.ShapeDtypeStruct((t, d), F32),
        compiler_params=_cparams(("parallel", "arbitrary"), 48),
        name="ffn_down",
    )(act, w_out)


def _ple_kernel(f_ref, x1_ref, p_ref, g_ref, wg_ref, wp_ref, o_ref):
    for r0 in range(0, f_ref.shape[0], ROW_SUB):
        rows = slice(r0, r0 + ROW_SUB)
        x2 = x1_ref[rows, :] + _rms(f_ref[rows, :], g_ref[...])
        gate = jnp.dot(x2.astype(BF16), wg_ref[...], preferred_element_type=F32)
        ple = jnp.dot(p_ref[rows, :].astype(BF16), wp_ref[...], preferred_element_type=F32)
        o_ref[rows, :] = x2 + jax.nn.sigmoid(gate) * ple


def _ple(f, x1, p2d, g_post, w_gate, w_proj, *, tm=512):
    t, d = x1.shape
    k = p2d.shape[1]
    return pl.pallas_call(
        _ple_kernel,
        grid=(t // tm,),
        in_specs=[
            pl.BlockSpec((tm, d), lambda i: (i, 0)),
            pl.BlockSpec((tm, d), lambda i: (i, 0)),
            pl.BlockSpec((tm, k), lambda i: (i, 0)),
            pl.BlockSpec((1, d), lambda i: (0, 0)),
            pl.BlockSpec((d, d), lambda i: (0, 0)),
            pl.BlockSpec((k, d), lambda i: (0, 0)),
        ],
        out_specs=pl.BlockSpec((tm, d), lambda i: (i, 0)),
        out_shape=jax.ShapeDtypeStruct((t, d), F32),
        compiler_params=_cparams(("parallel",), 52),
        name="ple",
    )(f, x1, p2d, g_post, w_gate, w_proj)


def kernel(x, p, w_in, conv_w, conv_b, dt_bias, a_log, d_skip, ssm_norm_g, w_ssm_out,
           w_pool_grp, pool_scale, w_pool_out, w_mix_out, g_pre_mix, g_post_mix,
           g_pre_ffn, g_post_ffn, w_ffn_in, w_ffn_out, w_ple_proj, w_ple_gate):
    batch, seq, d = x.shape
    depth = w_in.shape[0]
    d_inner = w_ssm_out.shape[1]
    d_pool = w_pool_out.shape[1]
    heads = dt_bias.shape[1]
    groups = SSM_GROUPS
    hpg = heads // groups
    gw = d_inner // groups
    n_bc = groups * D_STATE
    conv_dim = d_inner + 2 * n_bc
    t = batch * seq
    assert heads * SSM_HEAD_DIM == d_inner and conv_w.shape[2] == conv_dim
    assert w_in.shape[2] == d_inner + conv_dim + heads + d_pool + 2 * d

    dt_col0 = d_inner + conv_dim
    pool_col0 = dt_col0
    gs_col0 = pool_col0 + d_pool
    gp_col0 = gs_col0 + d

    x2d = x.reshape(t, d)
    for li in range(depth):
        wi = w_in[li]
        w_t = wi.T.astype(BF16)

        def per_group_cols(v):
            xs = v[..., :d_inner].reshape(v.shape[:-1] + (groups, gw))
            bm = v[..., d_inner:d_inner + n_bc].reshape(v.shape[:-1] + (groups, D_STATE))
            cm = v[..., d_inner + n_bc:].reshape(v.shape[:-1] + (groups, D_STATE))
            cat = jnp.concatenate([xs, bm, cm], axis=-1)
            return jnp.moveaxis(cat, -2, 0)
        cw = per_group_cols(0.5 * conv_w[li])
        cb = per_group_cols(0.5 * conv_b[li][None, :])
        dtb = dt_bias[li].reshape(groups, hpg, 1)
        alog = a_log[li].reshape(groups, hpg, 1)
        dsk = jnp.repeat(d_skip[li], SSM_HEAD_DIM).reshape(groups, 1, gw)
        ng = ssm_norm_g[li].reshape(groups, 1, gw)

        proj, dt_t = _in_proj(x2d, g_pre_mix[li][None, :], w_t, dt_row0=dt_col0,
                              n_half_cols=d_inner)
        y_ssm = _ssd(proj, dt_t, cw, cb, dtb, alog, dsk, ng,
                     batch=batch, seq=seq, d_inner=d_inner)
        y_pool = _pool(proj, w_pool_grp[li].astype(BF16), pool_scale[li][None, :],
                       seq=seq, d_pool=d_pool, col0=pool_col0)
        merged = _merge(y_ssm, y_pool, proj, w_ssm_out[li].astype(BF16),
                        w_pool_out[li].astype(BF16), gs_col0=gs_col0, gp_col0=gp_col0)
        x1, h2 = _mix(merged, x2d, w_mix_out[li].astype(BF16),
                      g_post_mix[li][None, :], g_pre_ffn[li][None, :])
        act = _ffn_up(h2, w_ffn_in[li].astype(BF16))
        f = _ffn_down(act, w_ffn_out[li].astype(BF16))
        x2d = _ple(f, x1, p[li].reshape(t, -1), g_post_ffn[li][None, :],
                   w_ple_gate[li].astype(BF16), w_ple_proj[li].astype(BF16))
    return x2d.reshape(batch, seq, d)
```
